```python
import math
import jax, jax.numpy as jnp
from jax import lax
import numpy as np

D_MODEL = 2048
BATCH = 8
SEQ = 2048
DEPTH = 1

MIX_WIDTH = D_MODEL
CONV_WIDTH = MIX_WIDTH // 2
MLSTM_WIDTH = MIX_WIDTH - CONV_WIDTH
MLSTM_HEADS = 4
MLSTM_HEAD_DIM = MLSTM_WIDTH // MLSTM_HEADS
CONV_K = 3
CHUNK = 64
D_FF = 4 * D_MODEL
EPS = 1e-6
IN_COLS = 3 * CONV_WIDTH + 4 * MLSTM_WIDTH + 2 * MLSTM_HEADS

kernel_name = "hybrid_conv_mlstm_parallel_heads"


def rms_norm(x, g):
    xf = x.astype(jnp.float32)
    y = xf * lax.rsqrt(jnp.mean(xf * xf, axis=-1, keepdims=True) + EPS)
    return (y * g.astype(jnp.float32)).astype(x.dtype)


def short_conv_mixer(cx, cb, cc, conv_w):
    u = cc * cx
    w = conv_w.astype(u.dtype).reshape(CONV_K, 1, CONV_WIDTH)
    y = lax.conv_general_dilated(
        u, w, window_strides=(1,), padding=[(CONV_K - 1, 0)],
        dimension_numbers=("NWC", "WIO", "NWC"),
        feature_group_count=CONV_WIDTH)
    return cb * y


def mlstm_chunkwise(q, k, v, logi, logf):
    b_, s_, h_, d_ = q.shape
    nc = s_ // CHUNK

    def to_chunks(t):
        return t.reshape(b_, nc, CHUNK, h_, d_).transpose(1, 0, 3, 2, 4)

    def gate_chunks(t):
        return t.reshape(b_, nc, CHUNK, h_).transpose(1, 0, 3, 2)

    qc = to_chunks(q * (d_ ** -0.5))
    kc, vc = to_chunks(k), to_chunks(v)
    ic, fc = gate_chunks(logi), gate_chunks(logf)
    causal = jnp.tril(jnp.ones((CHUNK, CHUNK), dtype=bool))

    def step(carry, inp):
        c_st, n_st, m_st = carry
        qj, kj, vj, li, lf = inp
        bcum = jnp.cumsum(lf, axis=-1)
        log_d = bcum[..., :, None] - bcum[..., None, :] + li[..., None, :]
        log_d = jnp.where(causal, log_d, -jnp.inf)
        m_inter = bcum + m_st[..., None]
        m_row = jnp.maximum(m_inter, jnp.max(log_d, axis=-1))
        dmat = jnp.exp(log_d - m_row[..., None])
        scores = jnp.einsum("bhjd,bhsd->bhjs", qj, kj) * dmat
        inter = jnp.exp(m_inter - m_row)
        num = jnp.einsum("bhjs,bhsd->bhjd", scores, vj) + \
            inter[..., None] * jnp.einsum("bhjd,bhde->bhje", qj, c_st)
        den = jnp.sum(scores, axis=-1) + inter * jnp.einsum("bhjd,bhd->bhj", qj, n_st)
        hj = num / jnp.maximum(jnp.abs(den), jnp.exp(-m_row))[..., None]
        b_last = bcum[..., -1]
        log_w = b_last[..., None] - bcum + li
        m_new = jnp.maximum(b_last + m_st, jnp.max(log_w, axis=-1))
        wgt = jnp.exp(log_w - m_new[..., None])
        decay = jnp.exp(b_last + m_st - m_new)
        c_new = decay[..., None, None] * c_st + jnp.einsum("bhs,bhsd,bhse->bhde", wgt, kj, vj)
        n_new = decay[..., None] * n_st + jnp.einsum("bhs,bhsd->bhd", wgt, kj)
        return (c_new, n_new, m_new), hj

    init = (jnp.zeros((b_, h_, d_, d_), jnp.float32),
            jnp.zeros((b_, h_, d_), jnp.float32),
            jnp.zeros((b_, h_), jnp.float32))
    _, hs = lax.scan(step, init, (qc, kc, vc, ic, fc))
    return hs.transpose(1, 0, 3, 2, 4).reshape(b_, s_, h_, d_)


def hybrid_layer(x, g_pre_mix, g_post_mix, g_pre_mlp, g_post_mlp, w_in, conv_w,
                 b_i, b_f, g_head, w_out, w_mlp1, w_mlp2):
    bsz, seq, _ = x.shape
    u = rms_norm(x, g_pre_mix)
    proj = jnp.einsum("bsd,de->bse", u, w_in)
    c0 = 3 * CONV_WIDTH
    cx = proj[..., 0:CONV_WIDTH]
    cb = proj[..., CONV_WIDTH:2 * CONV_WIDTH]
    cc = proj[..., 2 * CONV_WIDTH:c0]
    q = proj[..., c0:c0 + MLSTM_WIDTH]
    k = proj[..., c0 + MLSTM_WIDTH:c0 + 2 * MLSTM_WIDTH]
    v = proj[..., c0 + 2 * MLSTM_WIDTH:c0 + 3 * MLSTM_WIDTH]
    o = proj[..., c0 + 3 * MLSTM_WIDTH:c0 + 4 * MLSTM_WIDTH]
    g0 = c0 + 4 * MLSTM_WIDTH
    i_pre = proj[..., g0:g0 + MLSTM_HEADS]
    f_pre = proj[..., g0 + MLSTM_HEADS:g0 + 2 * MLSTM_HEADS]

    y_conv = short_conv_mixer(cx, cb, cc, conv_w)

    hd = (bsz, seq, MLSTM_HEADS, MLSTM_HEAD_DIM)
    logi = i_pre.astype(jnp.float32) + b_i.astype(jnp.float32)
    logf = jax.nn.log_sigmoid(f_pre.astype(jnp.float32) + b_f.astype(jnp.float32))
    h = mlstm_chunkwise(q.astype(jnp.float32).reshape(hd), k.astype(jnp.float32).reshape(hd),
                        v.astype(jnp.float32).reshape(hd), logi, logf)
    h = rms_norm(h, g_head.reshape(MLSTM_HEADS, MLSTM_HEAD_DIM)).reshape(bsz, seq, MLSTM_WIDTH)
    y_mlstm = (jax.nn.sigmoid(o.astype(jnp.float32)) * h).astype(x.dtype)

    y = jnp.concatenate([y_conv.astype(x.dtype), y_mlstm], axis=-1)
    y = jnp.einsum("bse,ed->bsd", y, w_out)
    x = x + rms_norm(y, g_post_mix)

    z = rms_norm(x, g_pre_mlp)
    z = jnp.square(jax.nn.relu(jnp.einsum("bsd,df->bsf", z, w_mlp1)))
    z = jnp.einsum("bsf,fd->bsd", z, w_mlp2)
    return x + rms_norm(z, g_post_mlp)


def setup_inputs(seed: int = 0) -> dict:
    key = jax.random.key(seed)
    ks = jax.random.split(key, 16)
    L = DEPTH

    def gain(k_, n):
        return 1.0 + 0.02 * jax.random.normal(k_, (L, n), jnp.float32)

    return {
        "x": jax.random.normal(ks[0], (BATCH, SEQ, D_MODEL), jnp.float32),
        "g_pre_mix": gain(ks[1], D_MODEL),
        "g_post_mix": gain(ks[2], D_MODEL),
        "g_pre_mlp": gain(ks[3], D_MODEL),
        "g_post_mlp": gain(ks[4], D_MODEL),
        "w_in": jax.random.normal(ks[5], (L, D_MODEL, IN_COLS), jnp.float32) * D_MODEL ** -0.5,
        "conv_w": jax.random.normal(ks[6], (L, CONV_K, CONV_WIDTH), jnp.float32) * CONV_K ** -0.5,
        "b_i": 0.1 * jax.random.normal(ks[7], (L, MLSTM_HEADS), jnp.float32) - 1.0,
        "b_f": 3.0 + 0.5 * jax.random.normal(ks[8], (L, MLSTM_HEADS), jnp.float32),
        "g_head": gain(ks[9], MLSTM_WIDTH),
        "w_out": jax.random.normal(ks[10], (L, MIX_WIDTH, D_MODEL), jnp.float32) * MIX_WIDTH ** -0.5,
        "w_mlp1": jax.random.normal(ks[11], (L, D_MODEL, D_FF), jnp.float32) * D_MODEL ** -0.5,
        "w_mlp2": jax.random.normal(ks[12], (L, D_FF, D_MODEL), jnp.float32) * D_FF ** -0.5,
    }


def reference(x, g_pre_mix, g_post_mix, g_pre_mlp, g_post_mlp, w_in, conv_w,
              b_i, b_f, g_head, w_out, w_mlp1, w_mlp2):
    for layer in range(DEPTH):
        x = hybrid_layer(x, g_pre_mix[layer], g_post_mix[layer], g_pre_mlp[layer],
                         g_post_mlp[layer], w_in[layer], conv_w[layer], b_i[layer],
                         b_f[layer], g_head[layer], w_out[layer], w_mlp1[layer],
                         w_mlp2[layer])
    return x
```

```python
import functools

import jax
import jax.numpy as jnp
from jax import lax
from jax.experimental import pallas as pl
from jax.experimental.pallas import tpu as pltpu

EPS = 1e-6
CONV_K = 3
N_HEADS = 4
LANES = 128

F32 = jnp.float32
BF16 = jnp.bfloat16

MIB = 1024 * 1024


def _rms_scale(t):
    return lax.rsqrt(jnp.mean(t * t, axis=-1, keepdims=True) + EPS)


def _in_proj_kernel(x_ref, g_ref, w_ref, wg_ref, p_ref, gate_ref, u_scr):
    @pl.when(pl.program_id(1) == 0)
    def _():
        x = x_ref[...]
        u = (x * _rms_scale(x) * g_ref[...]).astype(BF16)
        u_scr[...] = u
        gate_ref[...] = jnp.dot(u, wg_ref[...], preferred_element_type=F32)

    p_ref[...] = jnp.dot(u_scr[...], w_ref[...], preferred_element_type=F32).astype(BF16)


def _in_proj(x2, g, w_main, w_gate, *, tm, tn):
    m, d = x2.shape
    n = w_main.shape[1]
    return pl.pallas_call(
        _in_proj_kernel,
        name="in_proj",
        grid=(m // tm, n // tn),
        in_specs=[
            pl.BlockSpec((tm, d), lambda i, j: (i, 0)),
            pl.BlockSpec((1, d), lambda i, j: (0, 0)),
            pl.BlockSpec((d, tn), lambda i, j: (0, j)),
            pl.BlockSpec((d, LANES), lambda i, j: (0, 0)),
        ],
        out_specs=[
            pl.BlockSpec((tm, tn), lambda i, j: (i, j)),
            pl.BlockSpec((tm, LANES), lambda i, j: (i, 0)),
        ],
        out_shape=[
            jax.ShapeDtypeStruct((m, n), BF16),
            jax.ShapeDtypeStruct((m, LANES), F32),
        ],
        scratch_shapes=[pltpu.VMEM((tm, d), BF16)],
        compiler_params=pltpu.CompilerParams(
            dimension_semantics=("arbitrary", "arbitrary"),
            vmem_limit_bytes=48 * MIB),
    )(x2, g, w_main, w_gate)


def _split3(a):
    hi = a.astype(BF16)
    r1 = a - hi.astype(F32)
    mid = r1.astype(BF16)
    lo = (r1 - mid.astype(F32)).astype(BF16)
    return hi, mid, lo


def _mixer_kernel(p_ref, gate_ref, convw_ref, bias_ref, ghead_ref, y_ref,
                  r1_scr, r2_scr, c_scr, n_scr, m_scr, *, cw, hd):
    L = p_ref.shape[0]
    mw = N_HEADS * hd

    @pl.when(pl.program_id(1) == 0)
    def _():
        r1_scr[...] = jnp.zeros_like(r1_scr)
        r2_scr[...] = jnp.zeros_like(r2_scr)
        c_scr[...] = jnp.zeros_like(c_scr)
        n_scr[...] = jnp.zeros_like(n_scr)
        m_scr[...] = jnp.zeros_like(m_scr)

    cx = p_ref[:, 0:cw].astype(F32)
    cb = p_ref[:, cw:2 * cw].astype(F32)
    cc = p_ref[:, 2 * cw:3 * cw].astype(F32)
    u = cc * cx
    r1 = pltpu.roll(u, 1, 0)
    r2 = pltpu.roll(u, 2, 0)
    row = lax.broadcasted_iota(jnp.int32, u.shape, 0)
    s1 = jnp.where(row < 1, r1_scr[...], r1)
    s2 = jnp.where(row < 2, r2_scr[...], r2)
    r1_scr[...] = r1
    r2_scr[...] = r2
    w = convw_ref[...]
    y_conv = cb * (w[0:1, :] * s2 + w[1:2, :] * s1 + w[2:3, :] * u)
    y_ref[:, 0:cw] = y_conv.astype(y_ref.dtype)

    a = gate_ref[...] + bias_ref[...]
    lane = lax.broadcasted_iota(jnp.int32, a.shape, 1)
    gl = jnp.where(lane < N_HEADS, a, jax.nn.log_sigmoid(a))
    ti = lax.broadcasted_iota(jnp.int32, (L, L), 0)
    si = lax.broadcasted_iota(jnp.int32, (L, L), 1)
    causal = ti >= si
    tril = causal.astype(BF16)
    hi, mid, lo = _split3(gl)
    bc = (jnp.dot(tril, hi, preferred_element_type=F32)
          + jnp.dot(tril, mid, preferred_element_type=F32)
          + jnp.dot(tril, lo, preferred_element_type=F32))
    gl_t = gl.T
    bc_t = bc.T

    q0, k0, v0, o0 = 3 * cw, 3 * cw + mw, 3 * cw + 2 * mw, 3 * cw + 3 * mw
    for h in range(N_HEADS):
        cs = slice(h * hd, (h + 1) * hd)
        q = p_ref[:, q0 + h * hd:q0 + (h + 1) * hd] * (hd ** -0.5)
        k = p_ref[:, k0 + h * hd:k0 + (h + 1) * hd]
        v = p_ref[:, v0 + h * hd:v0 + (h + 1) * hd]
        o = p_ref[:, o0 + h * hd:o0 + (h + 1) * hd].astype(F32)
        q = q.astype(BF16)

        li_row = gl_t[h:h + 1, :]
        li_col = gl[:, h:h + 1]
        b_row = bc_t[N_HEADS + h:N_HEADS + h + 1, :]
        b_col = bc[:, N_HEADS + h:N_HEADS + h + 1]
        b_last = bc[L - 1:L, N_HEADS + h:N_HEADS + h + 1]
        m_prev = m_scr[h, 0:1, 0:1]
        c_prev = c_scr[h]
        n_prev = n_scr[h, 0:1, :]

        log_d = jnp.where(causal, b_col - b_row + li_row, -jnp.inf)
        m_inter = b_col + m_prev
        m_row = jnp.maximum(m_inter, jnp.max(log_d, axis=-1, keepdims=True))
        dmat = jnp.exp(log_d - m_row)
        s = lax.dot_general(q, k, (((1,), (1,)), ((), ())), preferred_element_type=F32)
        scores = s * dmat
        inter = jnp.exp(m_inter - m_row)
        qc = jnp.dot(q, c_prev.astype(BF16), preferred_element_type=F32)
        num = jnp.dot(scores.astype(BF16), v, preferred_element_type=F32) + inter * qc
        qn = jnp.sum(q.astype(F32) * n_prev, axis=-1, keepdims=True)
        den = jnp.sum(scores, axis=-1, keepdims=True) + inter * qn
        hj = num / jnp.maximum(jnp.abs(den), jnp.exp(-m_row))

        m_new = jnp.maximum(b_last + m_prev,
                            jnp.max(b_last - b_row + li_row, axis=-1, keepdims=True))
        wgt_col = jnp.exp(b_last - b_col + li_col - m_new)
        decay = jnp.exp(b_last + m_prev - m_new)
        kw = k.astype(F32) * wgt_col
        c_scr[h] = decay * c_prev + jnp.dot(kw.T.astype(BF16), v, preferred_element_type=F32)
        n_scr[h] = jnp.broadcast_to(decay * n_prev + jnp.sum(kw, axis=0, keepdims=True),
                                    n_scr.shape[1:])
        m_scr[h] = jnp.broadcast_to(m_new, m_scr.shape[1:])

        hn = hj * _rms_scale(hj) * ghead_ref[:, cs]
        y_ref[:, cw + h * hd:cw + (h + 1) * hd] = (jax.nn.sigmoid(o) * hn).astype(y_ref.dtype)


def _mixer(proj, gates, conv_w, bias, g_head, *, batch, seq, chunk, cw, hd):
    m, n = proj.shape
    mw = N_HEADS * hd
    nc = seq // chunk
    kern = functools.partial(_mixer_kernel, cw=cw, hd=hd)
    return pl.pallas_call(
        kern,
        name="mixer",
        grid=(batch, nc),
        in_specs=[
            pl.BlockSpec((chunk, n), lambda b, j: (b * nc + j, 0)),
            pl.BlockSpec((chunk, LANES), lambda b, j: (b * nc + j, 0)),
            pl.BlockSpec((CONV_K, cw), lambda b, j: (0, 0)),
            pl.BlockSpec((1, LANES), lambda b, j: (0, 0)),
            pl.BlockSpec((1, mw), lambda b, j: (0, 0)),
        ],
        out_specs=pl.BlockSpec((chunk, cw + mw), lambda b, j: (b * nc + j, 0)),
        out_shape=jax.ShapeDtypeStruct((m, cw + mw), BF16),
        scratch_shapes=[
            pltpu.VMEM((chunk, cw), F32),
            pltpu.VMEM((chunk, cw), F32),
            pltpu.VMEM((N_HEADS, hd, hd), F32),
            pltpu.VMEM((N_HEADS, 8, hd), F32),
            pltpu.VMEM((N_HEADS, 8, LANES), F32),
        ],
        compiler_params=pltpu.CompilerParams(
            dimension_semantics=("arbitrary", "arbitrary"),
            vmem_limit_bytes=48 * MIB),
    )(proj, gates, conv_w, bias, g_head)


def _out_proj_kernel(y_ref, w_ref, x_ref, g_ref, o_ref):
    t = jnp.dot(y_ref[...], w_ref[...], preferred_element_type=F32)
    o_ref[...] = x_ref[...] + t * _rms_scale(t) * g_ref[...]


def _out_proj(y, w_out, x2, g, *, tm):
    m, d = x2.shape
    e = y.shape[1]
    return pl.pallas_call(
        _out_proj_kernel,
        name="out_proj",
        grid=(m // tm,),
        in_specs=[
            pl.BlockSpec((tm, e), lambda i: (i, 0)),
            pl.BlockSpec((e, d), lambda i: (0, 0)),
            pl.BlockSpec((tm, d), lambda i: (i, 0)),
            pl.BlockSpec((1, d), lambda i: (0, 0)),
        ],
        out_specs=pl.BlockSpec((tm, d), lambda i: (i, 0)),
        out_shape=jax.ShapeDtypeStruct((m, d), F32),
        compiler_params=pltpu.CompilerParams(
            dimension_semantics=("arbitrary",),
            vmem_limit_bytes=48 * MIB),
    )(y, w_out, x2, g)


def _mlp_kernel(x_ref, g1_ref, w1_ref, w2_ref, g2_ref, o_ref, z_scr):
    f = pl.program_id(1)

    @pl.when(f == 0)
    def _():
        x = x_ref[...]
        z_scr[...] = (x * _rms_scale(x) * g1_ref[...]).astype(BF16)
        o_ref[...] = jnp.zeros_like(o_ref)

    h = jnp.dot(z_scr[...], w1_ref[...], preferred_element_type=F32)
    h = jnp.square(jnp.maximum(h, 0.0)).astype(BF16)
    o_ref[...] += jnp.dot(h, w2_ref[...], preferred_element_type=F32)

    @pl.when(f == pl.num_programs(1) - 1)
    def _():
        t = o_ref[...]
        o_ref[...] = x_ref[...] + t * _rms_scale(t) * g2_ref[...]


def _mlp(x1, g1, w1, w2, g2, *, tm, tf):
    m, d = x1.shape
    dff = w1.shape[1]
    return pl.pallas_call(
        _mlp_kernel,
        name="mlp",
        grid=(m // tm, dff // tf),
        in_specs=[
            pl.BlockSpec((tm, d), lambda i, f: (i, 0)),
            pl.BlockSpec((1, d), lambda i, f: (0, 0)),
            pl.BlockSpec((d, tf), lambda i, f: (0, f)),
            pl.BlockSpec((tf, d), lambda i, f: (f, 0)),
            pl.BlockSpec((1, d), lambda i, f: (0, 0)),
        ],
        out_specs=pl.BlockSpec((tm, d), lambda i, f: (i, 0)),
        out_shape=jax.ShapeDtypeStruct((m, d), F32),
        scratch_shapes=[pltpu.VMEM((tm, d), BF16)],
        compiler_params=pltpu.CompilerParams(
            dimension_semantics=("arbitrary", "arbitrary"),
            vmem_limit_bytes=56 * MIB),
    )(x1, g1, w1, w2, g2)


def _layer(x2, g_pre_mix, g_post_mix, g_pre_mlp, g_post_mlp, w_in, conv_w, b_i, b_f,
           g_head, w_out, w_mlp1, w_mlp2, *, batch, seq):
    d = x2.shape[1]
    cw = conv_w.shape[1]
    mw = g_head.shape[0]
    hd = mw // N_HEADS
    main = 3 * cw + 4 * mw
    n_gate = 2 * N_HEADS

    w_main = w_in[:, :main].astype(BF16)
    w_gate = jnp.pad(w_in[:, main:main + n_gate], ((0, 0), (0, LANES - n_gate))).astype(BF16)
    bias = jnp.pad(jnp.concatenate([b_i, b_f]).astype(F32), (0, LANES - n_gate)).reshape(1, LANES)

    proj, gates = _in_proj(x2, g_pre_mix.reshape(1, d), w_main, w_gate, tm=1024, tn=1024)
    y = _mixer(proj, gates, conv_w.astype(F32), bias, g_head.reshape(1, mw).astype(F32),
               batch=batch, seq=seq, chunk=256, cw=cw, hd=hd)
    x1 = _out_proj(y, w_out.astype(BF16), x2, g_post_mix.reshape(1, d), tm=512)
    return _mlp(x1, g_pre_mlp.reshape(1, d), w_mlp1.astype(BF16), w_mlp2.astype(BF16),
                g_post_mlp.reshape(1, d), tm=1024, tf=512)


def kernel(x, g_pre_mix, g_post_mix, g_pre_mlp, g_post_mlp, w_in, conv_w, b_i, b_f, g_head,
           w_out, w_mlp1, w_mlp2):
    batch, seq, d = x.shape
    x2 = x.reshape(batch * seq, d)
    for layer in range(w_in.shape[0]):
        x2 = _layer(x2, g_pre_mix[layer], g_post_mix[layer], g_pre_mlp[layer], g_post_mlp[layer],
                    w_in[layer], conv_w[layer], b_i[layer], b_f[layer], g_head[layer],
                    w_out[layer], w_mlp1[layer], w_mlp2[layer], batch=batch, seq=seq)
    return x2.reshape(batch, seq, d)
```

```python
import functools

import jax
import jax.numpy as jnp
from jax import lax
from jax.experimental import pallas as pl
from jax.experimental.pallas import tpu as pltpu

EPS = 1e-6
CONV_K = 3
N_HEADS = 4
LANES = 128

F32 = jnp.float32
BF16 = jnp.bfloat16

MIB = 1024 * 1024


def _rms_scale(t):
    return lax.rsqrt(jnp.mean(t * t, axis=-1, keepdims=True) + EPS)


def _in_proj_kernel(x_ref, g_ref, w_ref, wg_ref, p_ref, gate_ref, u_scr):
    @pl.when(pl.program_id(1) == 0)
    def _():
        x = x_ref[...]
        u = (x * _rms_scale(x) * g_ref[...]).astype(BF16)
        u_scr[...] = u
        gate_ref[...] = jnp.dot(u, wg_ref[...], preferred_element_type=F32)

    p_ref[...] = jnp.dot(u_scr[...], w_ref[...].astype(BF16),
                         preferred_element_type=F32).astype(BF16)


def _in_proj(x2, g, w_in, w_gate, *, n, tm, tn):
    m, d = x2.shape
    return pl.pallas_call(
        _in_proj_kernel,
        name="in_proj",
        grid=(m // tm, n // tn),
        in_specs=[
            pl.BlockSpec((tm, d), lambda i, j: (i, 0)),
            pl.BlockSpec((1, d), lambda i, j: (0, 0)),
            pl.BlockSpec((d, tn), lambda i, j: (0, j)),
            pl.BlockSpec((d, LANES), lambda i, j: (0, 0)),
        ],
        out_specs=[
            pl.BlockSpec((tm, tn), lambda i, j: (i, j)),
            pl.BlockSpec((tm, LANES), lambda i, j: (i, 0)),
        ],
        out_shape=[
            jax.ShapeDtypeStruct((m, n), BF16),
            jax.ShapeDtypeStruct((m, LANES), F32),
        ],
        scratch_shapes=[pltpu.VMEM((tm, d), BF16)],
        compiler_params=pltpu.CompilerParams(
            dimension_semantics=("arbitrary", "arbitrary"),
            vmem_limit_bytes=56 * MIB),
    )(x2, g, w_in, w_gate)


def _split3(a):
    hi = a.astype(BF16)
    r1 = a - hi.astype(F32)
    mid = r1.astype(BF16)
    lo = (r1 - mid.astype(F32)).astype(BF16)
    return hi, mid, lo


def _mixer_kernel(p_ref, gate_ref, convw_ref, bias_ref, ghead_ref, y_ref,
                  r1_scr, r2_scr, c_scr, n_scr, m_scr, *, cw, hd):
    L = p_ref.shape[0]
    mw = N_HEADS * hd

    @pl.when(pl.program_id(1) == 0)
    def _():
        r1_scr[...] = jnp.zeros_like(r1_scr)
        r2_scr[...] = jnp.zeros_like(r2_scr)
        c_scr[...] = jnp.zeros_like(c_scr)
        n_scr[...] = jnp.zeros_like(n_scr)
        m_scr[...] = jnp.zeros_like(m_scr)

    cx = p_ref[:, 0:cw].astype(F32)
    cb = p_ref[:, cw:2 * cw].astype(F32)
    cc = p_ref[:, 2 * cw:3 * cw].astype(F32)
    u = cc * cx
    r1 = pltpu.roll(u, 1, 0)
    r2 = pltpu.roll(u, 2, 0)
    row = lax.broadcasted_iota(jnp.int32, u.shape, 0)
    s1 = jnp.where(row < 1, r1_scr[...], r1)
    s2 = jnp.where(row < 2, r2_scr[...], r2)
    r1_scr[...] = r1
    r2_scr[...] = r2
    w = convw_ref[...]
    y_conv = cb * (w[0:1, :] * s2 + w[1:2, :] * s1 + w[2:3, :] * u)
    y_ref[:, 0:cw] = y_conv.astype(y_ref.dtype)

    a = gate_ref[...] + bias_ref[...]
    lane = lax.broadcasted_iota(jnp.int32, a.shape, 1)
    gl = jnp.where(lane < N_HEADS, a, jax.nn.log_sigmoid(a))
    ti = lax.broadcasted_iota(jnp.int32, (L, L), 0)
    si = lax.broadcasted_iota(jnp.int32, (L, L), 1)
    causal = ti >= si
    tril = causal.astype(BF16)
    hi, mid, lo = _split3(gl)
    bc = (jnp.dot(tril, hi, preferred_element_type=F32)
          + jnp.dot(tril, mid, preferred_element_type=F32)
          + jnp.dot(tril, lo, preferred_element_type=F32))
    gl_t = gl.T
    bc_t = bc.T

    q0, k0, v0, o0 = 3 * cw, 3 * cw + mw, 3 * cw + 2 * mw, 3 * cw + 3 * mw
    for h in range(N_HEADS):
        cs = slice(h * hd, (h + 1) * hd)
        q = p_ref[:, q0 + h * hd:q0 + (h + 1) * hd] * (hd ** -0.5)
        k = p_ref[:, k0 + h * hd:k0 + (h + 1) * hd]
        v = p_ref[:, v0 + h * hd:v0 + (h + 1) * hd]
        o = p_ref[:, o0 + h * hd:o0 + (h + 1) * hd].astype(F32)
        q = q.astype(BF16)

        li_row = gl_t[h:h + 1, :]
        li_col = gl[:, h:h + 1]
        b_row = bc_t[N_HEADS + h:N_HEADS + h + 1, :]
        b_col = bc[:, N_HEADS + h:N_HEADS + h + 1]
        b_last = bc[L - 1:L, N_HEADS + h:N_HEADS + h + 1]
        m_prev = m_scr[h, 0:1, 0:1]
        c_prev = c_scr[h]
        n_prev = n_scr[h, 0:1, :]

        log_d = jnp.where(causal, b_col - b_row + li_row, -jnp.inf)
        m_inter = b_col + m_prev
        m_row = jnp.maximum(m_inter, jnp.max(log_d, axis=-1, keepdims=True))
        dmat = jnp.exp(log_d - m_row)
        s = lax.dot_general(q, k, (((1,), (1,)), ((), ())), preferred_element_type=F32)
        scores = s * dmat
        inter = jnp.exp(m_inter - m_row)
        qc = jnp.dot(q, c_prev.astype(BF16), preferred_element_type=F32)
        num = jnp.dot(scores.astype(BF16), v, preferred_element_type=F32) + inter * qc
        qn = jnp.sum(q.astype(F32) * n_prev, axis=-1, keepdims=True)
        den = jnp.sum(scores, axis=-1, keepdims=True) + inter * qn
        hj = num / jnp.maximum(jnp.abs(den), jnp.exp(-m_row))

        m_new = jnp.maximum(b_last + m_prev,
                            jnp.max(b_last - b_row + li_row, axis=-1, keepdims=True))
        wgt_col = jnp.exp(b_last - b_col + li_col - m_new)
        decay = jnp.exp(b_last + m_prev - m_new)
        kw = k.astype(F32) * wgt_col
        c_scr[h] = decay * c_prev + jnp.dot(kw.T.astype(BF16), v, preferred_element_type=F32)
        n_scr[h] = jnp.broadcast_to(decay * n_prev + jnp.sum(kw, axis=0, keepdims=True),
                                    n_scr.shape[1:])
        m_scr[h] = jnp.broadcast_to(m_new, m_scr.shape[1:])

        hn = hj * _rms_scale(hj) * ghead_ref[:, cs]
        y_ref[:, cw + h * hd:cw + (h + 1) * hd] = (jax.nn.sigmoid(o) * hn).astype(y_ref.dtype)


def _mixer(proj, gates, conv_w, bias, g_head, *, batch, seq, chunk, cw, hd):
    m, n = proj.shape
    mw = N_HEADS * hd
    nc = seq // chunk
    kern = functools.partial(_mixer_kernel, cw=cw, hd=hd)
    return pl.pallas_call(
        kern,
        name="mixer",
        grid=(batch, nc),
        in_specs=[
            pl.BlockSpec((chunk, n), lambda b, j: (b * nc + j, 0)),
            pl.BlockSpec((chunk, LANES), lambda b, j: (b * nc + j, 0)),
            pl.BlockSpec((CONV_K, cw), lambda b, j: (0, 0)),
            pl.BlockSpec((1, LANES), lambda b, j: (0, 0)),
            pl.BlockSpec((1, mw), lambda b, j: (0, 0)),
        ],
        out_specs=pl.BlockSpec((chunk, cw + mw), lambda b, j: (b * nc + j, 0)),
        out_shape=jax.ShapeDtypeStruct((m, cw + mw), BF16),
        scratch_shapes=[
            pltpu.VMEM((chunk, cw), F32),
            pltpu.VMEM((chunk, cw), F32),
            pltpu.VMEM((N_HEADS, hd, hd), F32),
            pltpu.VMEM((N_HEADS, 8, hd), F32),
            pltpu.VMEM((N_HEADS, 8, LANES), F32),
        ],
        compiler_params=pltpu.CompilerParams(
            dimension_semantics=("arbitrary", "arbitrary"),
            vmem_limit_bytes=48 * MIB),
    )(proj, gates, conv_w, bias, g_head)


def _out_proj_kernel(y_ref, w_ref, x_ref, g_ref, o_ref, wb_scr):
    @pl.when(pl.program_id(0) == 0)
    def _():
        wb_scr[...] = w_ref[...].astype(BF16)

    t = jnp.dot(y_ref[...], wb_scr[...], preferred_element_type=F32)
    o_ref[...] = x_ref[...] + t * _rms_scale(t) * g_ref[...]


def _out_proj(y, w_out, x2, g, *, tm):
    m, d = x2.shape
    e = y.shape[1]
    return pl.pallas_call(
        _out_proj_kernel,
        name="out_proj",
        grid=(m // tm,),
        in_specs=[
            pl.BlockSpec((tm, e), lambda i: (i, 0)),
            pl.BlockSpec((e, d), lambda i: (0, 0), pipeline_mode=pl.Buffered(1)),
            pl.BlockSpec((tm, d), lambda i: (i, 0)),
            pl.BlockSpec((1, d), lambda i: (0, 0)),
        ],
        out_specs=pl.BlockSpec((tm, d), lambda i: (i, 0)),
        out_shape=jax.ShapeDtypeStruct((m, d), F32),
        scratch_shapes=[pltpu.VMEM((e, d), BF16)],
        compiler_params=pltpu.CompilerParams(
            dimension_semantics=("arbitrary",),
            vmem_limit_bytes=56 * MIB),
    )(y, w_out, x2, g)


def _mlp_kernel(x_ref, g1_ref, w1_ref, w2_ref, g2_ref, o_ref, z_scr):
    f = pl.program_id(1)

    @pl.when(f == 0)
    def _():
        x = x_ref[...]
        z_scr[...] = (x * _rms_scale(x) * g1_ref[...]).astype(BF16)
        o_ref[...] = jnp.zeros_like(o_ref)

    h = jnp.dot(z_scr[...], w1_ref[...].astype(BF16), preferred_element_type=F32)
    h = jnp.square(jnp.maximum(h, 0.0)).astype(BF16)
    o_ref[...] += jnp.dot(h, w2_ref[...].astype(BF16), preferred_element_type=F32)

    @pl.when(f == pl.num_programs(1) - 1)
    def _():
        t = o_ref[...]
        o_ref[...] = x_ref[...] + t * _rms_scale(t) * g2_ref[...]


def _mlp(x1, g1, w1, w2, g2, *, tm, tf):
    m, d = x1.shape
    dff = w1.shape[1]
    return pl.pallas_call(
        _mlp_kernel,
        name="mlp",
        grid=(m // tm, dff // tf),
        in_specs=[
            pl.BlockSpec((tm, d), lambda i, f: (i, 0), pipeline_mode=pl.Buffered(1)),
            pl.BlockSpec((1, d), lambda i, f: (0, 0)),
            pl.BlockSpec((d, tf), lambda i, f: (0, f)),
            pl.BlockSpec((tf, d), lambda i, f: (f, 0)),
            pl.BlockSpec((1, d), lambda i, f: (0, 0)),
        ],
        out_specs=pl.BlockSpec((tm, d), lambda i, f: (i, 0)),
        out_shape=jax.ShapeDtypeStruct((m, d), F32),
        scratch_shapes=[pltpu.VMEM((tm, d), BF16)],
        compiler_params=pltpu.CompilerParams(
            dimension_semantics=("arbitrary", "arbitrary"),
            vmem_limit_bytes=56 * MIB),
    )(x1, g1, w1, w2, g2)


def _layer(x2, g_pre_mix, g_post_mix, g_pre_mlp, g_post_mlp, w_in, conv_w, b_i, b_f,
           g_head, w_out, w_mlp1, w_mlp2, *, batch, seq):
    d = x2.shape[1]
    cw = conv_w.shape[1]
    mw = g_head.shape[0]
    hd = mw // N_HEADS
    main = 3 * cw + 4 * mw
    n_gate = 2 * N_HEADS

    w_gate = jnp.pad(w_in[:, main:main + n_gate], ((0, 0), (0, LANES - n_gate))).astype(BF16)
    bias = jnp.pad(jnp.concatenate([b_i, b_f]).astype(F32), (0, LANES - n_gate)).reshape(1, LANES)

    proj, gates = _in_proj(x2, g_pre_mix.reshape(1, d), w_in, w_gate, n=main, tm=1024, tn=1024)
    y = _mixer(proj, gates, conv_w.astype(F32), bias, g_head.reshape(1, mw).astype(F32),
               batch=batch, seq=seq, chunk=256, cw=cw, hd=hd)
    x1 = _out_proj(y, w_out, x2, g_post_mix.reshape(1, d), tm=512)
    return _mlp(x1, g_pre_mlp.reshape(1, d), w_mlp1, w_mlp2, g_post_mlp.reshape(1, d),
                tm=1024, tf=512)


def kernel(x, g_pre_mix, g_post_mix, g_pre_mlp, g_post_mlp, w_in, conv_w, b_i, b_f, g_head,
           w_out, w_mlp1, w_mlp2):
    batch, seq, d = x.shape
    x2 = x.reshape(batch * seq, d)
    for layer in range(w_in.shape[0]):
        x2 = _layer(x2, g_pre_mix[layer], g_post_mix[layer], g_pre_mlp[layer], g_post_mlp[layer],
                    w_in[layer], conv_w[layer], b_i[layer], b_f[layer], g_head[layer],
                    w_out[layer], w_mlp1[layer], w_mlp2[layer], batch=batch, seq=seq)
    return x2.reshape(batch, seq, d)
```

```python
import functools

import jax
import jax.numpy as jnp
from jax import lax
from jax.experimental import pallas as pl
from jax.experimental.pallas import tpu as pltpu

EPS = 1e-6
CONV_K = 3
N_HEADS = 4
LANES = 128

F32 = jnp.float32
BF16 = jnp.bfloat16

MIB = 1024 * 1024


def _rms_scale(t):
    return lax.rsqrt(jnp.mean(t * t, axis=-1, keepdims=True) + EPS)


def _dot_t(a, bt):
    return lax.dot_general(a, bt, (((1,), (1,)), ((), ())), preferred_element_type=F32)


def _in_proj_kernel(x_ref, g_ref, wt_ref, wgt_ref, p_ref, gate_ref, u_scr, *, sub):
    j = pl.program_id(1)

    @pl.when(j == 0)
    def _():
        for r in range(x_ref.shape[0] // sub):
            rows = slice(r * sub, (r + 1) * sub)
            x = x_ref[rows, :]
            u = (x * _rms_scale(x) * g_ref[...]).astype(BF16)
            u_scr[rows, :] = u
            gate_ref[rows, :] = _dot_t(u, wgt_ref[...])
            p_ref[rows, :] = _dot_t(u, wt_ref[...]).astype(BF16)

    @pl.when(j > 0)
    def _():
        p_ref[...] = _dot_t(u_scr[...], wt_ref[...]).astype(BF16)


def _in_proj(x2, g, w_main_t, w_gate_t, *, n, tm, tn, sub):
    m, d = x2.shape
    assert n % tn == 0 and n <= w_main_t.shape[0]
    return pl.pallas_call(
        functools.partial(_in_proj_kernel, sub=sub),
        name="in_proj",
        grid=(m // tm, n // tn),
        in_specs=[
            pl.BlockSpec((tm, d), lambda i, j: (i, 0)),
            pl.BlockSpec((1, d), lambda i, j: (0, 0)),
            pl.BlockSpec((tn, d), lambda i, j: (j, 0)),
            pl.BlockSpec((LANES, d), lambda i, j: (0, 0)),
        ],
        out_specs=[
            pl.BlockSpec((tm, tn), lambda i, j: (i, j)),
            pl.BlockSpec((tm, LANES), lambda i, j: (i, 0)),
        ],
        out_shape=[
            jax.ShapeDtypeStruct((m, n), BF16),
            jax.ShapeDtypeStruct((m, LANES), F32),
        ],
        scratch_shapes=[pltpu.VMEM((tm, d), BF16)],
        compiler_params=pltpu.CompilerParams(
            dimension_semantics=("arbitrary", "arbitrary"),
            vmem_limit_bytes=48 * MIB),
    )(x2, g, w_main_t, w_gate_t)


def _split3(a):
    hi = a.astype(BF16)
    r1 = a - hi.astype(F32)
    mid = r1.astype(BF16)
    lo = (r1 - mid.astype(F32)).astype(BF16)
    return hi, mid, lo


def _mixer_kernel(p_ref, gate_ref, convw_ref, bias_ref, ghead_ref, y_ref,
                  r1_scr, r2_scr, c_scr, n_scr, m_scr, *, cw, hd):
    L = p_ref.shape[0]
    mw = N_HEADS * hd

    @pl.when(pl.program_id(1) == 0)
    def _():
        r1_scr[...] = jnp.zeros_like(r1_scr)
        r2_scr[...] = jnp.zeros_like(r2_scr)
        c_scr[...] = jnp.zeros_like(c_scr)
        n_scr[...] = jnp.zeros_like(n_scr)
        m_scr[...] = jnp.zeros_like(m_scr)

    cx = p_ref[:, 0:cw].astype(F32)
    cb = p_ref[:, cw:2 * cw].astype(F32)
    cc = p_ref[:, 2 * cw:3 * cw].astype(F32)
    u = cc * cx
    r1 = pltpu.roll(u, 1, 0)
    r2 = pltpu.roll(u, 2, 0)
    row = lax.broadcasted_iota(jnp.int32, u.shape, 0)
    s1 = jnp.where(row < 1, r1_scr[...], r1)
    s2 = jnp.where(row < 2, r2_scr[...], r2)
    r1_scr[...] = r1
    r2_scr[...] = r2
    w = convw_ref[...]
    y_conv = cb * (w[0:1, :] * s2 + w[1:2, :] * s1 + w[2:3, :] * u)
    y_ref[:, 0:cw] = y_conv.astype(y_ref.dtype)

    a = gate_ref[...] + bias_ref[...]
    lane = lax.broadcasted_iota(jnp.int32, a.shape, 1)
    gl = jnp.where(lane < N_HEADS, a, jax.nn.log_sigmoid(a))
    ti = lax.broadcasted_iota(jnp.int32, (L, L), 0)
    si = lax.broadcasted_iota(jnp.int32, (L, L), 1)
    causal = ti >= si
    tril = causal.astype(BF16)
    hi, mid, lo = _split3(gl)
    bc = (jnp.dot(tril, hi, preferred_element_type=F32)
          + jnp.dot(tril, mid, preferred_element_type=F32)
          + jnp.dot(tril, lo, preferred_element_type=F32))
    gl_t = gl.T
    bc_t = bc.T

    q0, k0, v0, o0 = 3 * cw, 3 * cw + mw, 3 * cw + 2 * mw, 3 * cw + 3 * mw
    for h in range(N_HEADS):
        cs = slice(h * hd, (h + 1) * hd)
        q = p_ref[:, q0 + h * hd:q0 + (h + 1) * hd] * (hd ** -0.5)
        k = p_ref[:, k0 + h * hd:k0 + (h + 1) * hd]
        v = p_ref[:, v0 + h * hd:v0 + (h + 1) * hd]
        o = p_ref[:, o0 + h * hd:o0 + (h + 1) * hd].astype(F32)
        q = q.astype(BF16)

        li_row = gl_t[h:h + 1, :]
        li_col = gl[:, h:h + 1]
        b_row = bc_t[N_HEADS + h:N_HEADS + h + 1, :]
        b_col = bc[:, N_HEADS + h:N_HEADS + h + 1]
        b_last = bc[L - 1:L, N_HEADS + h:N_HEADS + h + 1]
        m_prev = m_scr[h, 0:1, 0:1]
        c_prev = c_scr[h]
        n_prev = n_scr[h, 0:1, :]

        log_d = jnp.where(causal, b_col - b_row + li_row, -jnp.inf)
        m_inter = b_col + m_prev
        m_row = jnp.maximum(m_inter, jnp.max(log_d, axis=-1, keepdims=True))
        dmat = jnp.exp(log_d - m_row)
        s = lax.dot_general(q, k, (((1,), (1,)), ((), ())), preferred_element_type=F32)
        scores = s * dmat
        inter = jnp.exp(m_inter - m_row)
        qc = jnp.dot(q, c_prev.astype(BF16), preferred_element_type=F32)
        num = jnp.dot(scores.astype(BF16), v, preferred_element_type=F32) + inter * qc
        qn = jnp.sum(q.astype(F32) * n_prev, axis=-1, keepdims=True)
        den = jnp.sum(scores, axis=-1, keepdims=True) + inter * qn
        hj = num / jnp.maximum(jnp.abs(den), jnp.exp(-m_row))

        m_new = jnp.maximum(b_last + m_prev,
                            jnp.max(b_last - b_row + li_row, axis=-1, keepdims=True))
        wgt_col = jnp.exp(b_last - b_col + li_col - m_new)
        decay = jnp.exp(b_last + m_prev - m_new)
        kw = k.astype(F32) * wgt_col
        c_scr[h] = decay * c_prev + jnp.dot(kw.T.astype(BF16), v, preferred_element_type=F32)
        n_scr[h] = jnp.broadcast_to(decay * n_prev + jnp.sum(kw, axis=0, keepdims=True),
                                    n_scr.shape[1:])
        m_scr[h] = jnp.broadcast_to(m_new, m_scr.shape[1:])

        hn = hj * _rms_scale(hj) * ghead_ref[:, cs]
        y_ref[:, cw + h * hd:cw + (h + 1) * hd] = (jax.nn.sigmoid(o) * hn).astype(y_ref.dtype)


def _mixer(proj, gates, conv_w, bias, g_head, *, batch, seq, chunk, cw, hd):
    m, n = proj.shape
    mw = N_HEADS * hd
    nc = seq // chunk
    kern = functools.partial(_mixer_kernel, cw=cw, hd=hd)
    return pl.pallas_call(
        kern,
        name="mixer",
        grid=(batch, nc),
        in_specs=[
            pl.BlockSpec((chunk, n), lambda b, j: (b * nc + j, 0)),
            pl.BlockSpec((chunk, LANES), lambda b, j: (b * nc + j, 0)),
            pl.BlockSpec((CONV_K, cw), lambda b, j: (0, 0)),
            pl.BlockSpec((1, LANES), lambda b, j: (0, 0)),
            pl.BlockSpec((1, mw), lambda b, j: (0, 0)),
        ],
        out_specs=pl.BlockSpec((chunk, cw + mw), lambda b, j: (b * nc + j, 0)),
        out_shape=jax.ShapeDtypeStruct((m, cw + mw), BF16),
        scratch_shapes=[
            pltpu.VMEM((chunk, cw), F32),
            pltpu.VMEM((chunk, cw), F32),
            pltpu.VMEM((N_HEADS, hd, hd), F32),
            pltpu.VMEM((N_HEADS, 8, hd), F32),
            pltpu.VMEM((N_HEADS, 8, LANES), F32),
        ],
        compiler_params=pltpu.CompilerParams(
            dimension_semantics=("arbitrary", "arbitrary"),
            vmem_limit_bytes=48 * MIB),
    )(proj, gates, conv_w, bias, g_head)


def _out_proj_kernel(y_ref, w_ref, x_ref, g_ref, o_ref, wb_scr):
    @pl.when(pl.program_id(0) == 0)
    def _():
        wb_scr[...] = w_ref[...].astype(BF16)

    t = jnp.dot(y_ref[...], wb_scr[...], preferred_element_type=F32)
    o_ref[...] = x_ref[...] + t * _rms_scale(t) * g_ref[...]


def _out_proj(y, w_out, x2, g, *, tm):
    m, d = x2.shape
    e = y.shape[1]
    return pl.pallas_call(
        _out_proj_kernel,
        name="out_proj",
        grid=(m // tm,),
        in_specs=[
            pl.BlockSpec((tm, e), lambda i: (i, 0)),
            pl.BlockSpec((e, d), lambda i: (0, 0), pipeline_mode=pl.Buffered(1)),
            pl.BlockSpec((tm, d), lambda i: (i, 0)),
            pl.BlockSpec((1, d), lambda i: (0, 0)),
        ],
        out_specs=pl.BlockSpec((tm, d), lambda i: (i, 0)),
        out_shape=jax.ShapeDtypeStruct((m, d), F32),
        scratch_shapes=[pltpu.VMEM((e, d), BF16)],
        compiler_params=pltpu.CompilerParams(
            dimension_semantics=("arbitrary",),
            vmem_limit_bytes=56 * MIB),
    )(y, w_out, x2, g)


def _mlp_kernel(x_ref, g1_ref, w1_ref, w2_ref, g2_ref, o_ref, z_scr, *, sub):
    f = pl.program_id(1)
    last = pl.num_programs(1) - 1
    n_sub = x_ref.shape[0] // sub

    def ffn(z):
        h = jnp.dot(z, w1_ref[...], preferred_element_type=F32)
        h = jnp.square(jnp.maximum(h, 0.0)).astype(BF16)
        return jnp.dot(h, w2_ref[...], preferred_element_type=F32)

    @pl.when(f == 0)
    def _():
        for r in range(n_sub):
            rows = slice(r * sub, (r + 1) * sub)
            x = x_ref[rows, :]
            z = (x * _rms_scale(x) * g1_ref[...]).astype(BF16)
            z_scr[rows, :] = z
            o_ref[rows, :] = ffn(z)

    @pl.when(jnp.logical_and(f > 0, f < last))
    def _():
        o_ref[...] += ffn(z_scr[...])

    @pl.when(f == last)
    def _():
        for r in range(n_sub):
            rows = slice(r * sub, (r + 1) * sub)
            t = o_ref[rows, :] + ffn(z_scr[rows, :])
            o_ref[rows, :] = x_ref[rows, :] + t * _rms_scale(t) * g2_ref[...]


def _mlp(x1, g1, w1, w2, g2, *, tm, tf, sub):
    m, d = x1.shape
    dff = w1.shape[1]
    assert dff // tf >= 2
    return pl.pallas_call(
        functools.partial(_mlp_kernel, sub=sub),
        name="mlp",
        grid=(m // tm, dff // tf),
        in_specs=[
            pl.BlockSpec((tm, d), lambda i, f: (i, 0)),
            pl.BlockSpec((1, d), lambda i, f: (0, 0)),
            pl.BlockSpec((d, tf), lambda i, f: (0, f)),
            pl.BlockSpec((tf, d), lambda i, f: (f, 0)),
            pl.BlockSpec((1, d), lambda i, f: (0, 0)),
        ],
        out_specs=pl.BlockSpec((tm, d), lambda i, f: (i, 0)),
        out_shape=jax.ShapeDtypeStruct((m, d), F32),
        scratch_shapes=[pltpu.VMEM((tm, d), BF16)],
        compiler_params=pltpu.CompilerParams(
            dimension_semantics=("arbitrary", "arbitrary"),
            vmem_limit_bytes=56 * MIB),
    )(x1, g1, w1, w2, g2)


def _layer(x2, g_pre_mix, g_post_mix, g_pre_mlp, g_post_mlp, w_in, conv_w, b_i, b_f,
           g_head, w_out, w_mlp1, w_mlp2, *, batch, seq):
    d = x2.shape[1]
    cw = conv_w.shape[1]
    mw = g_head.shape[0]
    hd = mw // N_HEADS
    main = 3 * cw + 4 * mw
    n_gate = 2 * N_HEADS

    w_t = jnp.swapaxes(w_in, 0, 1).astype(BF16)
    w_gate_t = jnp.pad(w_t[main:main + n_gate], ((0, LANES - n_gate), (0, 0)))
    bias = jnp.pad(jnp.concatenate([b_i, b_f]).astype(F32), (0, LANES - n_gate)).reshape(1, LANES)

    proj, gates = _in_proj(x2, g_pre_mix.reshape(1, d), w_t, w_gate_t, n=main, tm=1024, tn=1024,
                           sub=256)
    y = _mixer(proj, gates, conv_w.astype(F32), bias, g_head.reshape(1, mw).astype(F32),
               batch=batch, seq=seq, chunk=256, cw=cw, hd=hd)
    x1 = _out_proj(y, w_out, x2, g_post_mix.reshape(1, d), tm=512)
    return _mlp(x1, g_pre_mlp.reshape(1, d), w_mlp1.astype(BF16), w_mlp2.astype(BF16),
                g_post_mlp.reshape(1, d), tm=1024, tf=512, sub=256)


def kernel(x, g_pre_mix, g_post_mix, g_pre_mlp, g_post_mlp, w_in, conv_w, b_i, b_f, g_head,
           w_out, w_mlp1, w_mlp2):
    batch, seq, d = x.shape
    x2 = x.reshape(batch * seq, d)
    for layer in range(w_in.shape[0]):
        x2 = _layer(x2, g_pre_mix[layer], g_post_mix[layer], g_pre_mlp[layer], g_post_mlp[layer],
                    w_in[layer], conv_w[layer], b_i[layer], b_f[layer], g_head[layer],
                    w_out[layer], w_mlp1[layer], w_mlp2[layer], batch=batch, seq=seq)
    return x2.reshape(batch, seq, d)
```

```python
import functools

import jax
import jax.numpy as jnp
from jax import lax
from jax.experimental import pallas as pl
from jax.experimental.pallas import tpu as pltpu

EPS = 1e-6
CONV_K = 3
N_HEADS = 4
LANES = 128

F32 = jnp.float32
BF16 = jnp.bfloat16

MIB = 1024 * 1024


def _rms_scale(t):
    return lax.rsqrt(jnp.mean(t * t, axis=-1, keepdims=True) + EPS)


def _dot_t(a, bt):
    return lax.dot_general(a, bt, (((1,), (1,)), ((), ())), preferred_element_type=F32)


def _in_proj_kernel(x_ref, g_ref, wt_ref, wgt_ref, p_ref, gate_ref, u_scr, *, sub):
    j = pl.program_id(1)

    @pl.when(j == 0)
    def _():
        for r in range(x_ref.shape[0] // sub):
            rows = slice(r * sub, (r + 1) * sub)
            x = x_ref[rows, :]
            u = (x * _rms_scale(x) * g_ref[...]).astype(BF16)
            u_scr[rows, :] = u
            gate_ref[rows, :] = _dot_t(u, wgt_ref[...])
            p_ref[rows, :] = _dot_t(u, wt_ref[...]).astype(BF16)

    @pl.when(j > 0)
    def _():
        p_ref[...] = _dot_t(u_scr[...], wt_ref[...]).astype(BF16)


def _in_proj(x2, g, w_main_t, w_gate_t, *, n, tm, tn, sub):
    m, d = x2.shape
    assert n % tn == 0 and n <= w_main_t.shape[0]
    return pl.pallas_call(
        functools.partial(_in_proj_kernel, sub=sub),
        name="in_proj",
        grid=(m // tm, n // tn),
        in_specs=[
            pl.BlockSpec((tm, d), lambda i, j: (i, 0)),
            pl.BlockSpec((1, d), lambda i, j: (0, 0)),
            pl.BlockSpec((tn, d), lambda i, j: (j, 0)),
            pl.BlockSpec((LANES, d), lambda i, j: (0, 0)),
        ],
        out_specs=[
            pl.BlockSpec((tm, tn), lambda i, j: (i, j)),
            pl.BlockSpec((tm, LANES), lambda i, j: (i, 0)),
        ],
        out_shape=[
            jax.ShapeDtypeStruct((m, n), BF16),
            jax.ShapeDtypeStruct((m, LANES), F32),
        ],
        scratch_shapes=[pltpu.VMEM((tm, d), BF16)],
        compiler_params=pltpu.CompilerParams(
            dimension_semantics=("arbitrary", "arbitrary"),
            vmem_limit_bytes=48 * MIB),
    )(x2, g, w_main_t, w_gate_t)


def _split3(a):
    hi = a.astype(BF16)
    r1 = a - hi.astype(F32)
    mid = r1.astype(BF16)
    lo = (r1 - mid.astype(F32)).astype(BF16)
    return hi, mid, lo


def _mixer_kernel(p_ref, gate_ref, convw_ref, bias_ref, ghead_ref, y_ref,
                  c1_scr, c2_scr, ct_scr, m_scr, *, cw, hd):
    L = p_ref.shape[0]
    mw = N_HEADS * hd

    @pl.when(pl.program_id(1) == 0)
    def _():
        c1_scr[...] = jnp.zeros_like(c1_scr)
        c2_scr[...] = jnp.zeros_like(c2_scr)
        ct_scr[...] = jnp.zeros_like(ct_scr)
        m_scr[...] = jnp.zeros_like(m_scr)

    cx = p_ref[:, 0:cw].astype(F32)
    cb = p_ref[:, cw:2 * cw].astype(F32)
    cc = p_ref[:, 2 * cw:3 * cw].astype(F32)
    u = cc * cx
    r1 = pltpu.roll(u, 1, 0)
    r2 = pltpu.roll(u, 2, 0)
    row8 = lax.broadcasted_iota(jnp.int32, (8, cw), 0)
    s1 = jnp.concatenate([jnp.where(row8 < 1, c1_scr[...], r1[0:8]), r1[8:]], axis=0)
    s2 = jnp.concatenate([jnp.where(row8 < 2, c2_scr[...], r2[0:8]), r2[8:]], axis=0)
    c1_scr[...] = r1[0:8]
    c2_scr[...] = r2[0:8]
    w = convw_ref[...]
    y_conv = cb * (w[0:1, :] * s2 + w[1:2, :] * s1 + w[2:3, :] * u)
    y_ref[:, 0:cw] = y_conv.astype(y_ref.dtype)

    a = gate_ref[...] + bias_ref[...]
    lane = lax.broadcasted_iota(jnp.int32, a.shape, 1)
    gl = jnp.where(lane < N_HEADS, a, jax.nn.log_sigmoid(a))
    i0 = lax.broadcasted_iota(jnp.int32, (L, L), 0)
    i1 = lax.broadcasted_iota(jnp.int32, (L, L), 1)
    tril = (i0 >= i1).astype(BF16)
    hi, mid, lo = _split3(gl)
    bc = (jnp.dot(tril, hi, preferred_element_type=F32)
          + jnp.dot(tril, mid, preferred_element_type=F32)
          + jnp.dot(tril, lo, preferred_element_type=F32))
    bsh = pltpu.roll(bc, LANES - N_HEADS, 1)
    ab = jnp.where(lane < N_HEADS, gl - bsh, 0.0)
    b_rows = bsh.T
    b_last = bsh[L - 1:L, :]
    m_prev_v = m_scr[0:1, :]
    m_new_v = b_last + jnp.maximum(m_prev_v, jnp.max(ab, axis=0, keepdims=True))
    decay_v = jnp.exp(b_last + m_prev_v - m_new_v)
    wgt = jnp.exp(ab + (b_last - m_new_v))
    m_scr[...] = jnp.broadcast_to(m_new_v, m_scr.shape)

    keep = i0 <= i1
    q0, k0, v0, o0 = 3 * cw, 3 * cw + mw, 3 * cw + 2 * mw, 3 * cw + 3 * mw
    for h in range(N_HEADS):
        cs = slice(h * hd, (h + 1) * hd)
        q = p_ref[:, q0 + h * hd:q0 + (h + 1) * hd]
        k = p_ref[:, k0 + h * hd:k0 + (h + 1) * hd]
        v = p_ref[:, v0 + h * hd:v0 + (h + 1) * hd]
        o = p_ref[:, o0 + h * hd:o0 + (h + 1) * hd].astype(F32)

        b_row = b_rows[h:h + 1, :]
        a_col = ab[:, h:h + 1]
        m_prev = m_prev_v[:, h:h + 1]
        ctn = ct_scr[h]

        m_inter = b_row + m_prev
        log_d = jnp.where(keep, a_col + b_row, -jnp.inf)
        m_row = jnp.maximum(m_inter, jnp.max(log_d, axis=0, keepdims=True))
        scores = _dot_t(k, q) * jnp.exp(log_d - m_row)
        inter = jnp.exp(m_inter - m_row)
        qc = _dot_t(ctn.astype(BF16), q)
        v_t = v.astype(F32).T.astype(BF16)
        num = jnp.dot(v_t, scores.astype(BF16), preferred_element_type=F32) + inter * qc[0:hd]
        den = jnp.sum(scores, axis=0, keepdims=True) + inter * qc[hd:hd + 1]
        floor = (hd ** 0.5) * jnp.exp(-m_row)
        h_t = num * (1.0 / jnp.maximum(jnp.abs(den), floor))
        hn_t = h_t * lax.rsqrt(jnp.mean(h_t * h_t, axis=0, keepdims=True) + EPS)
        y_h = jax.nn.sigmoid(o) * (hn_t.T * ghead_ref[:, cs])
        y_ref[:, cw + h * hd:cw + (h + 1) * hd] = y_h.astype(y_ref.dtype)

        decay = decay_v[:, h:h + 1]
        kw = k.astype(F32) * wgt[:, h:h + 1]
        ct_scr[h, 0:hd, :] = decay * ctn[0:hd] + jnp.dot(v_t, kw.astype(BF16),
                                                         preferred_element_type=F32)
        ct_scr[h, hd:hd + 1, :] = decay * ctn[hd:hd + 1] + jnp.sum(kw, axis=0, keepdims=True)


def _mixer(proj, gates, conv_w, bias, g_head, *, batch, seq, chunk, cw, hd):
    m, n = proj.shape
    mw = N_HEADS * hd
    nc = seq // chunk
    kern = functools.partial(_mixer_kernel, cw=cw, hd=hd)
    return pl.pallas_call(
        kern,
        name="mixer",
        grid=(batch, nc),
        in_specs=[
            pl.BlockSpec((chunk, n), lambda b, j: (b * nc + j, 0)),
            pl.BlockSpec((chunk, LANES), lambda b, j: (b * nc + j, 0)),
            pl.BlockSpec((CONV_K, cw), lambda b, j: (0, 0)),
            pl.BlockSpec((1, LANES), lambda b, j: (0, 0)),
            pl.BlockSpec((1, mw), lambda b, j: (0, 0)),
        ],
        out_specs=pl.BlockSpec((chunk, cw + mw), lambda b, j: (b * nc + j, 0)),
        out_shape=jax.ShapeDtypeStruct((m, cw + mw), BF16),
        scratch_shapes=[
            pltpu.VMEM((8, cw), F32),
            pltpu.VMEM((8, cw), F32),
            pltpu.VMEM((N_HEADS, hd + 16, hd), F32),
            pltpu.VMEM((8, LANES), F32),
        ],
        compiler_params=pltpu.CompilerParams(
            dimension_semantics=("arbitrary", "arbitrary"),
            vmem_limit_bytes=48 * MIB),
    )(proj, gates, conv_w, bias, g_head)


def _out_proj_kernel(y_ref, w_ref, x_ref, g_ref, o_ref, wb_scr, *, sub):
    @pl.when(pl.program_id(0) == 0)
    def _():
        wb_scr[...] = w_ref[...].astype(BF16)

    for r in range(y_ref.shape[0] // sub):
        rows = slice(r * sub, (r + 1) * sub)
        t = jnp.dot(y_ref[rows, :], wb_scr[...], preferred_element_type=F32)
        o_ref[rows, :] = x_ref[rows, :] + t * _rms_scale(t) * g_ref[...]


def _out_proj(y, w_out, x2, g, *, tm, sub):
    m, d = x2.shape
    e = y.shape[1]
    return pl.pallas_call(
        functools.partial(_out_proj_kernel, sub=sub),
        name="out_proj",
        grid=(m // tm,),
        in_specs=[
            pl.BlockSpec((tm, e), lambda i: (i, 0)),
            pl.BlockSpec((e, d), lambda i: (0, 0), pipeline_mode=pl.Buffered(1)),
            pl.BlockSpec((tm, d), lambda i: (i, 0)),
            pl.BlockSpec((1, d), lambda i: (0, 0)),
        ],
        out_specs=pl.BlockSpec((tm, d), lambda i: (i, 0)),
        out_shape=jax.ShapeDtypeStruct((m, d), F32),
        scratch_shapes=[pltpu.VMEM((e, d), BF16)],
        compiler_params=pltpu.CompilerParams(
            dimension_semantics=("arbitrary",),
            vmem_limit_bytes=56 * MIB),
    )(y, w_out, x2, g)


def _mlp_kernel(x_ref, g1_ref, w1_ref, w2_ref, g2_ref, o_ref, z_scr, *, sub):
    f = pl.program_id(1)
    last = pl.num_programs(1) - 1
    n_sub = x_ref.shape[0] // sub

    def ffn(z):
        h = jnp.dot(z, w1_ref[...], preferred_element_type=F32)
        h = jnp.square(jnp.maximum(h, 0.0)).astype(BF16)
        return jnp.dot(h, w2_ref[...], preferred_element_type=F32)

    @pl.when(f == 0)
    def _():
        for r in range(n_sub):
            rows = slice(r * sub, (r + 1) * sub)
            x = x_ref[rows, :]
            z = (x * _rms_scale(x) * g1_ref[...]).astype(BF16)
            z_scr[rows, :] = z
            o_ref[rows, :] = ffn(z)

    @pl.when(jnp.logical_and(f > 0, f < last))
    def _():
        o_ref[...] += ffn(z_scr[...])

    @pl.when(f == last)
    def _():
        for r in range(n_sub):
            rows = slice(r * sub, (r + 1) * sub)
            t = o_ref[rows, :] + ffn(z_scr[rows, :])
            o_ref[rows, :] = x_ref[rows, :] + t * _rms_scale(t) * g2_ref[...]


def _mlp(x1, g1, w1, w2, g2, *, tm, tf, sub):
    m, d = x1.shape
    dff = w1.shape[1]
    assert dff // tf >= 2
    return pl.pallas_call(
        functools.partial(_mlp_kernel, sub=sub),
        name="mlp",
        grid=(m // tm, dff // tf),
        in_specs=[
            pl.BlockSpec((tm, d), lambda i, f: (i, 0)),
            pl.BlockSpec((1, d), lambda i, f: (0, 0)),
            pl.BlockSpec((d, tf), lambda i, f: (0, f)),
            pl.BlockSpec((tf, d), lambda i, f: (f, 0)),
            pl.BlockSpec((1, d), lambda i, f: (0, 0)),
        ],
        out_specs=pl.BlockSpec((tm, d), lambda i, f: (i, 0)),
        out_shape=jax.ShapeDtypeStruct((m, d), F32),
        scratch_shapes=[pltpu.VMEM((tm, d), BF16)],
        compiler_params=pltpu.CompilerParams(
            dimension_semantics=("arbitrary", "arbitrary"),
            vmem_limit_bytes=56 * MIB),
    )(x1, g1, w1, w2, g2)


def _layer(x2, g_pre_mix, g_post_mix, g_pre_mlp, g_post_mlp, w_in, conv_w, b_i, b_f,
           g_head, w_out, w_mlp1, w_mlp2, *, batch, seq):
    d = x2.shape[1]
    cw = conv_w.shape[1]
    mw = g_head.shape[0]
    hd = mw // N_HEADS
    main = 3 * cw + 4 * mw
    n_gate = 2 * N_HEADS

    w_t = jnp.swapaxes(w_in, 0, 1).astype(BF16)
    w_gate_t = jnp.pad(w_t[main:main + n_gate], ((0, LANES - n_gate), (0, 0)))
    bias = jnp.pad(jnp.concatenate([b_i, b_f]).astype(F32), (0, LANES - n_gate)).reshape(1, LANES)

    proj, gates = _in_proj(x2, g_pre_mix.reshape(1, d), w_t, w_gate_t, n=main, tm=1024, tn=1024,
                           sub=256)
    y = _mixer(proj, gates, conv_w.astype(F32), bias, g_head.reshape(1, mw).astype(F32),
               batch=batch, seq=seq, chunk=256, cw=cw, hd=hd)
    x1 = _out_proj(y, w_out, x2, g_post_mix.reshape(1, d), tm=512, sub=128)
    return _mlp(x1, g_pre_mlp.reshape(1, d), w_mlp1.astype(BF16), w_mlp2.astype(BF16),
                g_post_mlp.reshape(1, d), tm=1024, tf=512, sub=256)


def kernel(x, g_pre_mix, g_post_mix, g_pre_mlp, g_post_mlp, w_in, conv_w, b_i, b_f, g_head,
           w_out, w_mlp1, w_mlp2):
    batch, seq, d = x.shape
    x2 = x.reshape(batch * seq, d)
    for layer in range(w_in.shape[0]):
        x2 = _layer(x2, g_pre_mix[layer], g_post_mix[layer], g_pre_mlp[layer], g_post_mlp[layer],
                    w_in[layer], conv_w[layer], b_i[layer], b_f[layer], g_head[layer],
                    w_out[layer], w_mlp1[layer], w_mlp2[layer], batch=batch, seq=seq)
    return x2.reshape(batch, seq, d)
```

```python
import functools

import jax
import jax.numpy as jnp
from jax import lax
from jax.experimental import pallas as pl
from jax.experimental.pallas import tpu as pltpu

EPS = 1e-6
CONV_K = 3
N_HEADS = 4
LANES = 128

F32 = jnp.float32
BF16 = jnp.bfloat16

MIB = 1024 * 1024


def _rms_scale(t):
    return lax.rsqrt(jnp.mean(t * t, axis=-1, keepdims=True) + EPS)


def _dot_t(a, bt):
    return lax.dot_general(a, bt, (((1,), (1,)), ((), ())), preferred_element_type=F32)


def _in_proj_kernel(x_ref, g_ref, wt_ref, wgt_ref, p_ref, gate_ref, u_scr, *, sub):
    j = pl.program_id(1)

    @pl.when(j == 0)
    def _():
        for r in range(x_ref.shape[0] // sub):
            rows = slice(r * sub, (r + 1) * sub)
            x = x_ref[rows, :]
            u = (x * _rms_scale(x) * g_ref[...]).astype(BF16)
            u_scr[rows, :] = u
            gate_ref[rows, :] = _dot_t(u, wgt_ref[...])
            p_ref[rows, :] = _dot_t(u, wt_ref[...]).astype(BF16)

    @pl.when(j > 0)
    def _():
        p_ref[...] = _dot_t(u_scr[...], wt_ref[...]).astype(BF16)


def _in_proj(x2, g, w_main_t, w_gate_t, *, n, tm, tn, sub):
    m, d = x2.shape
    assert n % tn == 0 and n <= w_main_t.shape[0]
    return pl.pallas_call(
        functools.partial(_in_proj_kernel, sub=sub),
        name="in_proj",
        grid=(m // tm, n // tn),
        in_specs=[
            pl.BlockSpec((tm, d), lambda i, j: (i, 0)),
            pl.BlockSpec((1, d), lambda i, j: (0, 0)),
            pl.BlockSpec((tn, d), lambda i, j: (j, 0)),
            pl.BlockSpec((LANES, d), lambda i, j: (0, 0)),
        ],
        out_specs=[
            pl.BlockSpec((tm, tn), lambda i, j: (i, j)),
            pl.BlockSpec((tm, LANES), lambda i, j: (i, 0)),
        ],
        out_shape=[
            jax.ShapeDtypeStruct((m, n), BF16),
            jax.ShapeDtypeStruct((m, LANES), F32),
        ],
        scratch_shapes=[pltpu.VMEM((tm, d), BF16)],
        compiler_params=pltpu.CompilerParams(
            dimension_semantics=("arbitrary", "arbitrary"),
            vmem_limit_bytes=48 * MIB),
    )(x2, g, w_main_t, w_gate_t)


def _split3(a):
    hi = a.astype(BF16)
    r1 = a - hi.astype(F32)
    mid = r1.astype(BF16)
    lo = (r1 - mid.astype(F32)).astype(BF16)
    return hi, mid, lo


def _mixer_kernel(p_ref, gate_ref, convw_ref, bias_ref, ghead_ref, w1_ref, w2_ref,
                  y_ref, w1b_ref, w2b_ref, c1_scr, c2_scr, ct_scr, m_scr, *, cw, hd):
    L = p_ref.shape[0]
    mw = N_HEADS * hd

    w1b_ref[...] = w1_ref[...].astype(BF16)
    w2b_ref[...] = w2_ref[...].astype(BF16)

    @pl.when(pl.program_id(1) == 0)
    def _():
        c1_scr[...] = jnp.zeros_like(c1_scr)
        c2_scr[...] = jnp.zeros_like(c2_scr)
        ct_scr[...] = jnp.zeros_like(ct_scr)
        m_scr[...] = jnp.zeros_like(m_scr)

    cx = p_ref[:, 0:cw].astype(F32)
    cb = p_ref[:, cw:2 * cw].astype(F32)
    cc = p_ref[:, 2 * cw:3 * cw].astype(F32)
    u = cc * cx
    r1 = pltpu.roll(u, 1, 0)
    r2 = pltpu.roll(u, 2, 0)
    row8 = lax.broadcasted_iota(jnp.int32, (8, cw), 0)
    s1 = jnp.concatenate([jnp.where(row8 < 1, c1_scr[...], r1[0:8]), r1[8:]], axis=0)
    s2 = jnp.concatenate([jnp.where(row8 < 2, c2_scr[...], r2[0:8]), r2[8:]], axis=0)
    c1_scr[...] = r1[0:8]
    c2_scr[...] = r2[0:8]
    w = convw_ref[...]
    y_conv = cb * (w[0:1, :] * s2 + w[1:2, :] * s1 + w[2:3, :] * u)
    y_ref[:, 0:cw] = y_conv.astype(y_ref.dtype)

    a = gate_ref[...] + bias_ref[...]
    lane = lax.broadcasted_iota(jnp.int32, a.shape, 1)
    gl = jnp.where(lane < N_HEADS, a, jax.nn.log_sigmoid(a))
    i0 = lax.broadcasted_iota(jnp.int32, (L, L), 0)
    i1 = lax.broadcasted_iota(jnp.int32, (L, L), 1)
    tril = (i0 >= i1).astype(BF16)
    hi, mid, lo = _split3(gl)
    bc = (jnp.dot(tril, hi, preferred_element_type=F32)
          + jnp.dot(tril, mid, preferred_element_type=F32)
          + jnp.dot(tril, lo, preferred_element_type=F32))
    bsh = pltpu.roll(bc, LANES - N_HEADS, 1)
    ab = jnp.where(lane < N_HEADS, gl - bsh, 0.0)
    b_rows = bsh.T
    b_last = bsh[L - 1:L, :]
    m_prev_v = m_scr[0:1, :]
    m_new_v = b_last + jnp.maximum(m_prev_v, jnp.max(ab, axis=0, keepdims=True))
    decay_v = jnp.exp(b_last + m_prev_v - m_new_v)
    wgt = jnp.exp(ab + (b_last - m_new_v))
    m_scr[...] = jnp.broadcast_to(m_new_v, m_scr.shape)

    keep = i0 <= i1
    q0, k0, v0, o0 = 3 * cw, 3 * cw + mw, 3 * cw + 2 * mw, 3 * cw + 3 * mw
    for h in range(N_HEADS):
        cs = slice(h * hd, (h + 1) * hd)
        q = p_ref[:, q0 + h * hd:q0 + (h + 1) * hd]
        k = p_ref[:, k0 + h * hd:k0 + (h + 1) * hd]
        v = p_ref[:, v0 + h * hd:v0 + (h + 1) * hd]
        o = p_ref[:, o0 + h * hd:o0 + (h + 1) * hd].astype(F32)

        b_row = b_rows[h:h + 1, :]
        a_col = ab[:, h:h + 1]
        m_prev = m_prev_v[:, h:h + 1]
        ctn = ct_scr[h]

        m_inter = b_row + m_prev
        log_d = jnp.where(keep, a_col + b_row, -jnp.inf)
        m_row = jnp.maximum(m_inter, jnp.max(log_d, axis=0, keepdims=True))
        scores = _dot_t(k, q) * jnp.exp(log_d - m_row)
        inter = jnp.exp(m_inter - m_row)
        qc = _dot_t(ctn.astype(BF16), q)
        v_t = v.astype(F32).T.astype(BF16)
        num = jnp.dot(v_t, scores.astype(BF16), preferred_element_type=F32) + inter * qc[0:hd]
        den = jnp.sum(scores, axis=0, keepdims=True) + inter * qc[hd:hd + 1]
        floor = (hd ** 0.5) * jnp.exp(-m_row)
        h_t = num * (1.0 / jnp.maximum(jnp.abs(den), floor))
        hn_t = h_t * lax.rsqrt(jnp.mean(h_t * h_t, axis=0, keepdims=True) + EPS)
        y_h = jax.nn.sigmoid(o) * (hn_t.T * ghead_ref[:, cs])
        y_ref[:, cw + h * hd:cw + (h + 1) * hd] = y_h.astype(y_ref.dtype)

        decay = decay_v[:, h:h + 1]
        kw = k.astype(F32) * wgt[:, h:h + 1]
        ct_scr[h, 0:hd, :] = decay * ctn[0:hd] + jnp.dot(v_t, kw.astype(BF16),
                                                         preferred_element_type=F32)
        ct_scr[h, hd:hd + 1, :] = decay * ctn[hd:hd + 1] + jnp.sum(kw, axis=0, keepdims=True)


def _mixer(proj, gates, conv_w, bias, g_head, w1, w2, *, batch, seq, chunk, cw, hd):
    m, n = proj.shape
    mw = N_HEADS * hd
    nc = seq // chunk
    steps = batch * nc
    r1, r2 = w1.shape[0] // steps, w2.shape[0] // steps
    assert r1 * steps == w1.shape[0] and r2 * steps == w2.shape[0] and r1 % 16 == 0 and r2 % 16 == 0
    kern = functools.partial(_mixer_kernel, cw=cw, hd=hd)
    step_blk = lambda b, j: (b * nc + j, 0)
    return pl.pallas_call(
        kern,
        name="mixer",
        grid=(batch, nc),
        in_specs=[
            pl.BlockSpec((chunk, n), step_blk),
            pl.BlockSpec((chunk, LANES), step_blk),
            pl.BlockSpec((CONV_K, cw), lambda b, j: (0, 0)),
            pl.BlockSpec((1, LANES), lambda b, j: (0, 0)),
            pl.BlockSpec((1, mw), lambda b, j: (0, 0)),
            pl.BlockSpec((r1, w1.shape[1]), step_blk),
            pl.BlockSpec((r2, w2.shape[1]), step_blk),
        ],
        out_specs=[
            pl.BlockSpec((chunk, cw + mw), step_blk),
            pl.BlockSpec((r1, w1.shape[1]), step_blk),
            pl.BlockSpec((r2, w2.shape[1]), step_blk),
        ],
        out_shape=[
            jax.ShapeDtypeStruct((m, cw + mw), BF16),
            jax.ShapeDtypeStruct(w1.shape, BF16),
            jax.ShapeDtypeStruct(w2.shape, BF16),
        ],
        scratch_shapes=[
            pltpu.VMEM((8, cw), F32),
            pltpu.VMEM((8, cw), F32),
            pltpu.VMEM((N_HEADS, hd + 16, hd), F32),
            pltpu.VMEM((8, LANES), F32),
        ],
        compiler_params=pltpu.CompilerParams(
            dimension_semantics=("arbitrary", "arbitrary"),
            vmem_limit_bytes=48 * MIB),
    )(proj, gates, conv_w, bias, g_head, w1, w2)


def _out_proj_kernel(y_ref, w_ref, x_ref, g_ref, o_ref, wb_scr, *, sub):
    @pl.when(pl.program_id(0) == 0)
    def _():
        wb_scr[...] = w_ref[...].astype(BF16)

    for r in range(y_ref.shape[0] // sub):
        rows = slice(r * sub, (r + 1) * sub)
        t = jnp.dot(y_ref[rows, :], wb_scr[...], preferred_element_type=F32)
        o_ref[rows, :] = x_ref[rows, :] + t * _rms_scale(t) * g_ref[...]


def _out_proj(y, w_out, x2, g, *, tm, sub):
    m, d = x2.shape
    e = y.shape[1]
    return pl.pallas_call(
        functools.partial(_out_proj_kernel, sub=sub),
        name="out_proj",
        grid=(m // tm,),
        in_specs=[
            pl.BlockSpec((tm, e), lambda i: (i, 0)),
            pl.BlockSpec((e, d), lambda i: (0, 0), pipeline_mode=pl.Buffered(1)),
            pl.BlockSpec((tm, d), lambda i: (i, 0)),
            pl.BlockSpec((1, d), lambda i: (0, 0)),
        ],
        out_specs=pl.BlockSpec((tm, d), lambda i: (i, 0)),
        out_shape=jax.ShapeDtypeStruct((m, d), F32),
        scratch_shapes=[pltpu.VMEM((e, d), BF16)],
        compiler_params=pltpu.CompilerParams(
            dimension_semantics=("arbitrary",),
            vmem_limit_bytes=56 * MIB),
    )(y, w_out, x2, g)


def _mlp_kernel(x_ref, g1_ref, w1_ref, w2_ref, g2_ref, o_ref, z_scr, *, sub, mid_sub):
    f = pl.program_id(1)
    last = pl.num_programs(1) - 1
    n_sub = x_ref.shape[0] // sub

    def ffn(z):
        h = jnp.dot(z, w1_ref[...], preferred_element_type=F32)
        h = jnp.square(jnp.maximum(h, 0.0)).astype(BF16)
        return jnp.dot(h, w2_ref[...], preferred_element_type=F32)

    @pl.when(f == 0)
    def _():
        for r in range(n_sub):
            rows = slice(r * sub, (r + 1) * sub)
            x = x_ref[rows, :]
            z = (x * _rms_scale(x) * g1_ref[...]).astype(BF16)
            z_scr[rows, :] = z
            o_ref[rows, :] = ffn(z)

    @pl.when(jnp.logical_and(f > 0, f < last))
    def _():
        for r in range(x_ref.shape[0] // mid_sub):
            rows = slice(r * mid_sub, (r + 1) * mid_sub)
            o_ref[rows, :] += ffn(z_scr[rows, :])

    @pl.when(f == last)
    def _():
        for r in range(n_sub):
            rows = slice(r * sub, (r + 1) * sub)
            t = o_ref[rows, :] + ffn(z_scr[rows, :])
            o_ref[rows, :] = x_ref[rows, :] + t * _rms_scale(t) * g2_ref[...]


def _mlp(x1, g1, w1, w2, g2, *, tm, tf, sub, mid_sub, vmem_limit):
    m, d = x1.shape
    dff = w1.shape[1]
    assert dff // tf >= 2
    return pl.pallas_call(
        functools.partial(_mlp_kernel, sub=sub, mid_sub=mid_sub),
        name="mlp",
        grid=(m // tm, dff // tf),
        in_specs=[
            pl.BlockSpec((tm, d), lambda i, f: (i, 0)),
            pl.BlockSpec((1, d), lambda i, f: (0, 0)),
            pl.BlockSpec((d, tf), lambda i, f: (0, f)),
            pl.BlockSpec((tf, d), lambda i, f: (f, 0)),
            pl.BlockSpec((1, d), lambda i, f: (0, 0)),
        ],
        out_specs=pl.BlockSpec((tm, d), lambda i, f: (i, 0)),
        out_shape=jax.ShapeDtypeStruct((m, d), F32),
        scratch_shapes=[pltpu.VMEM((tm, d), BF16)],
        compiler_params=pltpu.CompilerParams(
            dimension_semantics=("arbitrary", "arbitrary"),
            vmem_limit_bytes=vmem_limit),
    )(x1, g1, w1, w2, g2)


def _layer(x2, g_pre_mix, g_post_mix, g_pre_mlp, g_post_mlp, w_in, conv_w, b_i, b_f,
           g_head, w_out, w_mlp1, w_mlp2, *, batch, seq):
    d = x2.shape[1]
    cw = conv_w.shape[1]
    mw = g_head.shape[0]
    hd = mw // N_HEADS
    main = 3 * cw + 4 * mw
    n_gate = 2 * N_HEADS

    w_t = jnp.swapaxes(w_in, 0, 1).astype(BF16)
    w_gate_t = jnp.pad(w_t[main:main + n_gate], ((0, LANES - n_gate), (0, 0)))
    bias = jnp.pad(jnp.concatenate([b_i, b_f]).astype(F32), (0, LANES - n_gate)).reshape(1, LANES)

    proj, gates = _in_proj(x2, g_pre_mix.reshape(1, d), w_t, w_gate_t, n=main, tm=1024, tn=1024,
                           sub=256)
    y, w1b, w2b = _mixer(proj, gates, conv_w.astype(F32), bias, g_head.reshape(1, mw).astype(F32),
                         w_mlp1, w_mlp2, batch=batch, seq=seq, chunk=256, cw=cw, hd=hd)
    x1 = _out_proj(y, w_out, x2, g_post_mix.reshape(1, d), tm=512, sub=512)
    return _mlp(x1, g_pre_mlp.reshape(1, d), w1b, w2b,
                g_post_mlp.reshape(1, d), tm=1024, tf=1024, sub=256, mid_sub=512,
                vmem_limit=60 * MIB)


def kernel(x, g_pre_mix, g_post_mix, g_pre_mlp, g_post_mlp, w_in, conv_w, b_i, b_f, g_head,
           w_out, w_mlp1, w_mlp2):
    batch, seq, d = x.shape
    x2 = x.reshape(batch * seq, d)
    for layer in range(w_in.shape[0]):
        x2 = _layer(x2, g_pre_mix[layer], g_post_mix[layer], g_pre_mlp[layer], g_post_mlp[layer],
                    w_in[layer], conv_w[layer], b_i[layer], b_f[layer], g_head[layer],
                    w_out[layer], w_mlp1[layer], w_mlp2[layer], batch=batch, seq=seq)
    return x2.reshape(batch, seq, d)
```

```python
import functools

import jax
import jax.numpy as jnp
from jax import lax
from jax.experimental import pallas as pl
from jax.experimental.pallas import tpu as pltpu

EPS = 1e-6
CONV_K = 3
N_HEADS = 4
LANES = 128
NORM_ROWS = 16

F32 = jnp.float32
BF16 = jnp.bfloat16

MIB = 1024 * 1024


def _rms_scale(t):
    return lax.rsqrt(jnp.mean(t * t, axis=-1, keepdims=True) + EPS)


def _dot_t(a, bt):
    return lax.dot_general(a, bt, (((1,), (1,)), ((), ())), preferred_element_type=F32)


def _merge_evenly(a, b):
    out, ia, ib = [], 0, 0
    while ia < len(a) or ib < len(b):
        if ib >= len(b) or (ia < len(a) and ia * len(b) <= ib * len(a)):
            out.append(a[ia])
            ia += 1
        else:
            out.append(b[ib])
            ib += 1
    return out


def _split3(a):
    hi = a.astype(BF16)
    r1 = a - hi.astype(F32)
    mid = r1.astype(BF16)
    lo = (r1 - mid.astype(F32)).astype(BF16)
    return hi, mid, lo


def _mix_stages(p_ref, gate_ref, convw_ref, bias_ref, ghead_ref, y_ref,
                c1_scr, c2_scr, ct_scr, m_scr, *, cw, hd):
    shared = {}

    def conv_stage():
        _mix_conv(p_ref, convw_ref, y_ref, c1_scr, c2_scr, cw=cw)

    def gate_stage():
        shared.update(_mix_gates(gate_ref, bias_ref, m_scr))

    def head_stage(h):
        _mix_head(h, p_ref, ghead_ref, y_ref, ct_scr, shared, cw=cw, hd=hd)

    return [conv_stage, gate_stage] + [functools.partial(head_stage, h) for h in range(N_HEADS)]


def _mix_conv(p_ref, convw_ref, y_ref, c1_scr, c2_scr, *, cw):
    row8 = lax.broadcasted_iota(jnp.int32, (8, LANES), 0)
    for c in range(cw // LANES):
        cols = slice(c * LANES, (c + 1) * LANES)
        cx = p_ref[:, c * LANES:(c + 1) * LANES].astype(F32)
        cb = p_ref[:, cw + c * LANES:cw + (c + 1) * LANES].astype(F32)
        cc = p_ref[:, 2 * cw + c * LANES:2 * cw + (c + 1) * LANES].astype(F32)
        u = cc * cx
        r1 = pltpu.roll(u, 1, 0)
        r2 = pltpu.roll(u, 2, 0)
        s1 = jnp.concatenate([jnp.where(row8 < 1, c1_scr[:, cols], r1[0:8]), r1[8:]], axis=0)
        s2 = jnp.concatenate([jnp.where(row8 < 2, c2_scr[:, cols], r2[0:8]), r2[8:]], axis=0)
        c1_scr[:, cols] = r1[0:8]
        c2_scr[:, cols] = r2[0:8]
        w = convw_ref[:, cols]
        y_conv = cb * (w[0:1, :] * s2 + w[1:2, :] * s1 + w[2:3, :] * u)
        y_ref[:, cols] = y_conv.astype(y_ref.dtype)


def _mix_gates(gate_ref, bias_ref, m_scr):
    L = gate_ref.shape[0]
    a = gate_ref[...] + bias_ref[...]
    lane = lax.broadcasted_iota(jnp.int32, a.shape, 1)
    gl = jnp.where(lane < N_HEADS, a, jax.nn.log_sigmoid(a))
    i0 = lax.broadcasted_iota(jnp.int32, (L, L), 0)
    i1 = lax.broadcasted_iota(jnp.int32, (L, L), 1)
    tril = (i0 >= i1).astype(BF16)
    hi, mid, lo = _split3(gl)
    bc = (jnp.dot(tril, hi, preferred_element_type=F32)
          + jnp.dot(tril, mid, preferred_element_type=F32)
          + jnp.dot(tril, lo, preferred_element_type=F32))
    bsh = pltpu.roll(bc, LANES - N_HEADS, 1)
    ab = jnp.where(lane < N_HEADS, gl - bsh, 0.0)
    b_rows = bsh.T
    b_last = bsh[L - 1:L, :]
    m_prev_v = m_scr[0:1, :]
    m_new_v = b_last + jnp.maximum(m_prev_v, jnp.max(ab, axis=0, keepdims=True))
    decay_v = jnp.exp(b_last + m_prev_v - m_new_v)
    wgt = jnp.exp(ab + (b_last - m_new_v))
    m_scr[...] = jnp.broadcast_to(m_new_v, m_scr.shape)
    return dict(ab=ab, b_rows=b_rows, m_prev_v=m_prev_v, decay_v=decay_v, wgt=wgt,
                keep=i0 <= i1)


def _mix_head(h, p_ref, ghead_ref, y_ref, ct_scr, shared, *, cw, hd):
    mw = N_HEADS * hd
    q0, k0, v0, o0 = 3 * cw, 3 * cw + mw, 3 * cw + 2 * mw, 3 * cw + 3 * mw
    cs = slice(h * hd, (h + 1) * hd)
    q = p_ref[:, q0 + h * hd:q0 + (h + 1) * hd]
    k = p_ref[:, k0 + h * hd:k0 + (h + 1) * hd]
    v = p_ref[:, v0 + h * hd:v0 + (h + 1) * hd]
    o = p_ref[:, o0 + h * hd:o0 + (h + 1) * hd].astype(F32)

    b_row = shared["b_rows"][h:h + 1, :]
    a_col = shared["ab"][:, h:h + 1]
    m_prev = shared["m_prev_v"][:, h:h + 1]
    ctn = ct_scr[h]

    m_inter = b_row + m_prev
    log_d = jnp.where(shared["keep"], a_col + b_row, -jnp.inf)
    m_row = jnp.maximum(m_inter, jnp.max(log_d, axis=0, keepdims=True))
    scores = _dot_t(k, q) * jnp.exp(log_d - m_row)
    inter = jnp.exp(m_inter - m_row)
    qc = _dot_t(ctn.astype(BF16), q)
    v_t = v.astype(F32).T.astype(BF16)
    num = jnp.dot(v_t, scores.astype(BF16), preferred_element_type=F32) + inter * qc[0:hd]
    den = jnp.sum(scores, axis=0, keepdims=True) + inter * qc[hd:hd + 1]
    floor = (hd ** 0.5) * jnp.exp(-m_row)
    h_t = num * (1.0 / jnp.maximum(jnp.abs(den), floor))
    hn_t = h_t * lax.rsqrt(jnp.mean(h_t * h_t, axis=0, keepdims=True) + EPS)
    y_h = jax.nn.sigmoid(o) * (hn_t.T * ghead_ref[:, cs])
    y_ref[:, cw + h * hd:cw + (h + 1) * hd] = y_h.astype(y_ref.dtype)

    decay = shared["decay_v"][:, h:h + 1]
    kw = k.astype(F32) * shared["wgt"][:, h:h + 1]
    ct_scr[h, 0:hd, :] = decay * ctn[0:hd] + jnp.dot(v_t, kw.astype(BF16),
                                                     preferred_element_type=F32)
    ct_scr[h, hd:hd + 1, :] = decay * ctn[hd:hd + 1] + jnp.sum(kw, axis=0, keepdims=True)


def _front_kernel(x_ref, g_ref, wt_ref, wgt_ref, convw_ref, bias_ref, ghead_ref, w1_ref, w2_ref,
                  y_ref, w1b_ref, w2b_ref,
                  u_a, u_b, p_a, p_b, g_a, g_b, c1_scr, c2_scr, ct_scr, m_scr,
                  *, n_chunks, nc, n_main, tn, cw, hd):
    s = pl.program_id(0)

    def norm_stages(u_scr):
        def norm(r):
            rows = slice(r * NORM_ROWS, (r + 1) * NORM_ROWS)
            x = x_ref[rows, :]
            u_scr[rows, :] = (x * _rms_scale(x) * g_ref[...]).astype(BF16)

        return [functools.partial(norm, r) for r in range(x_ref.shape[0] // NORM_ROWS)]

    def project_stages(u_scr, p_scr, g_scr):
        def gate_cols():
            g_scr[...] = _dot_t(u_scr[...], wgt_ref[...])

        def main_cols(c):
            cols = slice(c * tn, (c + 1) * tn)
            p_scr[:, cols] = _dot_t(u_scr[...], wt_ref[cols, :]).astype(BF16)

        return [gate_cols] + [functools.partial(main_cols, c) for c in range(n_main // tn)]

    def mix_stages(p_scr, g_scr):
        def cast_stage():
            w1b_ref[...] = w1_ref[...].astype(BF16)
            w2b_ref[...] = w2_ref[...].astype(BF16)

        return [cast_stage] + _mix_stages(p_scr, g_scr, convw_ref, bias_ref, ghead_ref, y_ref,
                                          c1_scr, c2_scr, ct_scr, m_scr, cw=cw, hd=hd)

    def run_interleaved(*streams):
        merged = []
        for st in streams:
            merged = _merge_evenly(merged, st)
        for stage in merged:
            stage()

    def reset_state_at_sequence_start():
        @pl.when(lax.rem(s - 2, nc) == 0)
        def _():
            c1_scr[...] = jnp.zeros_like(c1_scr)
            c2_scr[...] = jnp.zeros_like(c2_scr)
            ct_scr[...] = jnp.zeros_like(ct_scr)
            m_scr[...] = jnp.zeros_like(m_scr)

    even = lax.rem(s, 2) == 0
    steady = jnp.logical_and(s >= 2, s < n_chunks)

    @pl.when(s == 0)
    def _():
        run_interleaved(norm_stages(u_a))

    @pl.when(s == 1)
    def _():
        run_interleaved(project_stages(u_a, p_a, g_a), norm_stages(u_b))

    @pl.when(jnp.logical_and(steady, even))
    def _():
        reset_state_at_sequence_start()
        run_interleaved(project_stages(u_b, p_b, g_b), mix_stages(p_a, g_a), norm_stages(u_a))

    @pl.when(jnp.logical_and(steady, jnp.logical_not(even)))
    def _():
        reset_state_at_sequence_start()
        run_interleaved(project_stages(u_a, p_a, g_a), mix_stages(p_b, g_b), norm_stages(u_b))

    @pl.when(s == n_chunks)
    def _():
        reset_state_at_sequence_start()
        run_interleaved(project_stages(u_b, p_b, g_b), mix_stages(p_a, g_a))

    @pl.when(s == n_chunks + 1)
    def _():
        reset_state_at_sequence_start()
        run_interleaved(mix_stages(p_b, g_b))


def _front(x2, g, w_t, w_gate_t, conv_w, bias, g_head, w1, w2, *, n_main, chunk, seq, tn, cw, hd):
    m, d = x2.shape
    mw = N_HEADS * hd
    n_chunks = m // chunk
    nc = seq // chunk
    assert n_chunks % 2 == 0 and n_main % tn == 0 and n_main <= w_t.shape[0]
    r1, r2 = w1.shape[0] // n_chunks, w2.shape[0] // n_chunks
    assert r1 * n_chunks == w1.shape[0] and r2 * n_chunks == w2.shape[0] and r1 % 16 == 0 and r2 % 16 == 0
    norm_blk = lambda s: (jnp.minimum(s, n_chunks - 1), 0)
    mix_blk = lambda s: (jnp.clip(s - 2, 0, n_chunks - 1), 0)
    whole = lambda s: (0, 0)
    kern = functools.partial(_front_kernel, n_chunks=n_chunks, nc=nc, n_main=n_main, tn=tn,
                             cw=cw, hd=hd)
    return pl.pallas_call(
        kern,
        name="front",
        grid=(n_chunks + 2,),
        in_specs=[
            pl.BlockSpec((chunk, d), norm_blk),
            pl.BlockSpec((1, d), whole),
            pl.BlockSpec(w_t.shape, whole, pipeline_mode=pl.Buffered(1)),
            pl.BlockSpec((LANES, d), whole, pipeline_mode=pl.Buffered(1)),
            pl.BlockSpec((CONV_K, cw), whole),
            pl.BlockSpec((1, LANES), whole),
            pl.BlockSpec((1, mw), whole),
            pl.BlockSpec((r1, w1.shape[1]), mix_blk),
            pl.BlockSpec((r2, w2.shape[1]), mix_blk),
        ],
        out_specs=[
            pl.BlockSpec((chunk, cw + mw), mix_blk),
            pl.BlockSpec((r1, w1.shape[1]), mix_blk),
            pl.BlockSpec((r2, w2.shape[1]), mix_blk),
        ],
        out_shape=[
            jax.ShapeDtypeStruct((m, cw + mw), BF16),
            jax.ShapeDtypeStruct(w1.shape, BF16),
            jax.ShapeDtypeStruct(w2.shape, BF16),
        ],
        scratch_shapes=[
            pltpu.VMEM((chunk, d), BF16),
            pltpu.VMEM((chunk, d), BF16),
            pltpu.VMEM((chunk, n_main), BF16),
            pltpu.VMEM((chunk, n_main), BF16),
            pltpu.VMEM((chunk, LANES), F32),
            pltpu.VMEM((chunk, LANES), F32),
            pltpu.VMEM((8, cw), F32),
            pltpu.VMEM((8, cw), F32),
            pltpu.VMEM((N_HEADS, hd + 16, hd), F32),
            pltpu.VMEM((8, LANES), F32),
        ],
        compiler_params=pltpu.CompilerParams(
            dimension_semantics=("arbitrary",),
            vmem_limit_bytes=60 * MIB),
    )(x2, g, w_t, w_gate_t, conv_w, bias, g_head, w1, w2)


def _out_proj_kernel(y_ref, w_ref, x_ref, g_ref, o_ref, wb_scr):
    @pl.when(pl.program_id(0) == 0)
    def _():
        wb_scr[...] = w_ref[...].astype(BF16)

    t = jnp.dot(y_ref[...], wb_scr[...], preferred_element_type=F32)
    o_ref[...] = x_ref[...] + t * _rms_scale(t) * g_ref[...]


def _out_proj(y, w_out, x2, g, *, tm):
    m, d = x2.shape
    e = y.shape[1]
    return pl.pallas_call(
        _out_proj_kernel,
        name="out_proj",
        grid=(m // tm,),
        in_specs=[
            pl.BlockSpec((tm, e), lambda i: (i, 0)),
            pl.BlockSpec((e, d), lambda i: (0, 0), pipeline_mode=pl.Buffered(1)),
            pl.BlockSpec((tm, d), lambda i: (i, 0)),
            pl.BlockSpec((1, d), lambda i: (0, 0)),
        ],
        out_specs=pl.BlockSpec((tm, d), lambda i: (i, 0)),
        out_shape=jax.ShapeDtypeStruct((m, d), F32),
        scratch_shapes=[pltpu.VMEM((e, d), BF16)],
        compiler_params=pltpu.CompilerParams(
            dimension_semantics=("arbitrary",),
            vmem_limit_bytes=56 * MIB),
    )(y, w_out, x2, g)


def _mlp_kernel(x_ref, g1_ref, w1_ref, w2_ref, g2_ref, o_ref, z_scr, *, sub, mid_sub):
    f = pl.program_id(1)
    last = pl.num_programs(1) - 1
    n_sub = x_ref.shape[0] // sub

    def ffn(z):
        h = jnp.dot(z, w1_ref[...], preferred_element_type=F32)
        h = jnp.square(jnp.maximum(h, 0.0)).astype(BF16)
        return jnp.dot(h, w2_ref[...], preferred_element_type=F32)

    @pl.when(f == 0)
    def _():
        for r in range(n_sub):
            rows = slice(r * sub, (r + 1) * sub)
            x = x_ref[rows, :]
            z = (x * _rms_scale(x) * g1_ref[...]).astype(BF16)
            z_scr[rows, :] = z
            o_ref[rows, :] = ffn(z)

    @pl.when(jnp.logical_and(f > 0, f < last))
    def _():
        for r in range(x_ref.shape[0] // mid_sub):
            rows = slice(r * mid_sub, (r + 1) * mid_sub)
            o_ref[rows, :] += ffn(z_scr[rows, :])

    @pl.when(f == last)
    def _():
        for r in range(n_sub):
            rows = slice(r * sub, (r + 1) * sub)
            t = o_ref[rows, :] + ffn(z_scr[rows, :])
            o_ref[rows, :] = x_ref[rows, :] + t * _rms_scale(t) * g2_ref[...]


def _mlp(x1, g1, w1, w2, g2, *, tm, tf, sub, mid_sub, vmem_limit):
    m, d = x1.shape
    dff = w1.shape[1]
    assert dff // tf >= 2
    return pl.pallas_call(
        functools.partial(_mlp_kernel, sub=sub, mid_sub=mid_sub),
        name="mlp",
        grid=(m // tm, dff // tf),
        in_specs=[
            pl.BlockSpec((tm, d), lambda i, f: (i, 0)),
            pl.BlockSpec((1, d), lambda i, f: (0, 0)),
            pl.BlockSpec((d, tf), lambda i, f: (0, f)),
            pl.BlockSpec((tf, d), lambda i, f: (f, 0)),
            pl.BlockSpec((1, d), lambda i, f: (0, 0)),
        ],
        out_specs=pl.BlockSpec((tm, d), lambda i, f: (i, 0)),
        out_shape=jax.ShapeDtypeStruct((m, d), F32),
        scratch_shapes=[pltpu.VMEM((tm, d), BF16)],
        compiler_params=pltpu.CompilerParams(
            dimension_semantics=("arbitrary", "arbitrary"),
            vmem_limit_bytes=vmem_limit),
    )(x1, g1, w1, w2, g2)


def _layer(x2, g_pre_mix, g_post_mix, g_pre_mlp, g_post_mlp, w_in, conv_w, b_i, b_f,
           g_head, w_out, w_mlp1, w_mlp2, *, seq):
    d = x2.shape[1]
    cw = conv_w.shape[1]
    mw = g_head.shape[0]
    hd = mw // N_HEADS
    main = 3 * cw + 4 * mw
    n_gate = 2 * N_HEADS

    w_t = jnp.swapaxes(w_in, 0, 1).astype(BF16)
    w_gate_t = jnp.pad(w_t[main:main + n_gate], ((0, LANES - n_gate), (0, 0)))
    bias = jnp.pad(jnp.concatenate([b_i, b_f]).astype(F32), (0, LANES - n_gate)).reshape(1, LANES)

    y, w1b, w2b = _front(x2, g_pre_mix.reshape(1, d), w_t, w_gate_t, conv_w.astype(F32), bias,
                         g_head.reshape(1, mw).astype(F32), w_mlp1, w_mlp2,
                         n_main=main, chunk=256, seq=seq, tn=1024, cw=cw, hd=hd)
    x1 = _out_proj(y, w_out, x2, g_post_mix.reshape(1, d), tm=512)
    return _mlp(x1, g_pre_mlp.reshape(1, d), w1b, w2b, g_post_mlp.reshape(1, d),
                tm=1024, tf=1024, sub=256, mid_sub=512, vmem_limit=60 * MIB)


def kernel(x, g_pre_mix, g_post_mix, g_pre_mlp, g_post_mlp, w_in, conv_w, b_i, b_f, g_head,
           w_out, w_mlp1, w_mlp2):
    batch, seq, d = x.shape
    x2 = x.reshape(batch * seq, d)
    for layer in range(w_in.shape[0]):
        x2 = _layer(x2, g_pre_mix[layer], g_post_mix[layer], g_pre_mlp[layer], g_post_mlp[layer],
                    w_in[layer], conv_w[layer], b_i[layer], b_f[layer], g_head[layer],
                    w_out[layer], w_mlp1[layer], w_mlp2[layer], seq=seq)
    return x2.reshape(batch, seq, d)
```

```python
import functools

import jax
import jax.numpy as jnp
from jax import lax
from jax.experimental import pallas as pl
from jax.experimental.pallas import tpu as pltpu

EPS = 1e-6
CONV_K = 3
N_HEADS = 4
LANES = 128
NORM_ROWS = 16

F32 = jnp.float32
BF16 = jnp.bfloat16

MIB = 1024 * 1024


def _rms_scale(t):
    return lax.rsqrt(jnp.mean(t * t, axis=-1, keepdims=True) + EPS)


def _dot_t(a, bt):
    return lax.dot_general(a, bt, (((1,), (1,)), ((), ())), preferred_element_type=F32)


def _merge_evenly(a, b):
    out, ia, ib = [], 0, 0
    while ia < len(a) or ib < len(b):
        if ib >= len(b) or (ia < len(a) and ia * len(b) <= ib * len(a)):
            out.append(a[ia])
            ia += 1
        else:
            out.append(b[ib])
            ib += 1
    return out


def _split3(a):
    hi = a.astype(BF16)
    r1 = a - hi.astype(F32)
    mid = r1.astype(BF16)
    lo = (r1 - mid.astype(F32)).astype(BF16)
    return hi, mid, lo


def _mix_stages(p_ref, gate_ref, convw_ref, bias_ref, ghead_ref, y_ref,
                c1_scr, c2_scr, ct_scr, m_scr, *, cw, hd):
    shared = {}

    def conv_stage():
        _mix_conv(p_ref, convw_ref, y_ref, c1_scr, c2_scr, cw=cw)

    def gate_stage():
        shared.update(_mix_gates(gate_ref, bias_ref, m_scr))

    def head_stage(h):
        _mix_head(h, p_ref, ghead_ref, y_ref, ct_scr, shared, cw=cw, hd=hd)

    return [conv_stage, gate_stage] + [functools.partial(head_stage, h) for h in range(N_HEADS)]


def _mix_conv(p_ref, convw_ref, y_ref, c1_scr, c2_scr, *, cw):
    row8 = lax.broadcasted_iota(jnp.int32, (8, LANES), 0)
    for c in range(cw // LANES):
        cols = slice(c * LANES, (c + 1) * LANES)
        cx = p_ref[:, c * LANES:(c + 1) * LANES].astype(F32)
        cb = p_ref[:, cw + c * LANES:cw + (c + 1) * LANES].astype(F32)
        cc = p_ref[:, 2 * cw + c * LANES:2 * cw + (c + 1) * LANES].astype(F32)
        u = cc * cx
        r1 = pltpu.roll(u, 1, 0)
        r2 = pltpu.roll(u, 2, 0)
        s1 = jnp.concatenate([jnp.where(row8 < 1, c1_scr[:, cols], r1[0:8]), r1[8:]], axis=0)
        s2 = jnp.concatenate([jnp.where(row8 < 2, c2_scr[:, cols], r2[0:8]), r2[8:]], axis=0)
        c1_scr[:, cols] = r1[0:8]
        c2_scr[:, cols] = r2[0:8]
        w = convw_ref[:, cols]
        y_conv = cb * (w[0:1, :] * s2 + w[1:2, :] * s1 + w[2:3, :] * u)
        y_ref[:, cols] = y_conv.astype(y_ref.dtype)


def _mix_gates(gate_ref, bias_ref, m_scr):
    L = gate_ref.shape[0]
    a = gate_ref[...] + bias_ref[...]
    lane = lax.broadcasted_iota(jnp.int32, a.shape, 1)
    gl = jnp.where(lane < N_HEADS, a, jax.nn.log_sigmoid(a))
    i0 = lax.broadcasted_iota(jnp.int32, (L, L), 0)
    i1 = lax.broadcasted_iota(jnp.int32, (L, L), 1)
    tril = (i0 >= i1).astype(BF16)
    hi, mid, lo = _split3(gl)
    bc = (jnp.dot(tril, hi, preferred_element_type=F32)
          + jnp.dot(tril, mid, preferred_element_type=F32)
          + jnp.dot(tril, lo, preferred_element_type=F32))
    bsh = pltpu.roll(bc, LANES - N_HEADS, 1)
    ab = jnp.where(lane < N_HEADS, gl - bsh, 0.0)
    b_rows = bsh.T
    b_last = bsh[L - 1:L, :]
    m_prev_v = m_scr[0:1, :]
    m_new_v = b_last + jnp.maximum(m_prev_v, jnp.max(ab, axis=0, keepdims=True))
    decay_v = jnp.exp(b_last + m_prev_v - m_new_v)
    wgt = jnp.exp(ab + (b_last - m_new_v))
    m_scr[...] = jnp.broadcast_to(m_new_v, m_scr.shape)
    return dict(ab=ab, b_rows=b_rows, m_prev_v=m_prev_v, decay_v=decay_v, wgt=wgt,
                keep=i0 <= i1)


def _mix_head(h, p_ref, ghead_ref, y_ref, ct_scr, shared, *, cw, hd):
    mw = N_HEADS * hd
    q0, k0, v0, o0 = 3 * cw, 3 * cw + mw, 3 * cw + 2 * mw, 3 * cw + 3 * mw
    cs = slice(h * hd, (h + 1) * hd)
    q = p_ref[:, q0 + h * hd:q0 + (h + 1) * hd]
    k = p_ref[:, k0 + h * hd:k0 + (h + 1) * hd]
    v = p_ref[:, v0 + h * hd:v0 + (h + 1) * hd]
    o = p_ref[:, o0 + h * hd:o0 + (h + 1) * hd].astype(F32)

    b_row = shared["b_rows"][h:h + 1, :]
    a_col = shared["ab"][:, h:h + 1]
    m_prev = shared["m_prev_v"][:, h:h + 1]
    ctn = ct_scr[h]

    m_inter = b_row + m_prev
    log_d = jnp.where(shared["keep"], a_col + b_row, -jnp.inf)
    m_row = jnp.maximum(m_inter, jnp.max(log_d, axis=0, keepdims=True))
    scores = _dot_t(k, q) * jnp.exp(log_d - m_row)
    inter = jnp.exp(m_inter - m_row)
    qc = _dot_t(ctn.astype(BF16), q)
    v_t = v.astype(F32).T.astype(BF16)
    num = jnp.dot(v_t, scores.astype(BF16), preferred_element_type=F32) + inter * qc[0:hd]
    den = jnp.sum(scores, axis=0, keepdims=True) + inter * qc[hd:hd + 1]
    floor = (hd ** 0.5) * jnp.exp(-m_row)
    h_t = num * (1.0 / jnp.maximum(jnp.abs(den), floor))
    hn_t = h_t * lax.rsqrt(jnp.mean(h_t * h_t, axis=0, keepdims=True) + EPS)
    y_h = jax.nn.sigmoid(o) * (hn_t.T * ghead_ref[:, cs])
    y_ref[:, cw + h * hd:cw + (h + 1) * hd] = y_h.astype(y_ref.dtype)

    decay = shared["decay_v"][:, h:h + 1]
    kw = k.astype(F32) * shared["wgt"][:, h:h + 1]
    ct_scr[h, 0:hd, :] = decay * ctn[0:hd] + jnp.dot(v_t, kw.astype(BF16),
                                                     preferred_element_type=F32)
    ct_scr[h, hd:hd + 1, :] = decay * ctn[hd:hd + 1] + jnp.sum(kw, axis=0, keepdims=True)


def _front_kernel(x_ref, g_ref, wt_ref, wgt_ref, convw_ref, bias_ref, ghead_ref, w1_ref, w2_ref,
                  y_ref, w1b_ref, w2b_ref,
                  u_a, u_b, p_a, p_b, g_a, g_b, c1_scr, c2_scr, ct_scr, m_scr,
                  *, nc, n_main, tn, cw, hd):
    s = pl.program_id(0)

    def norm_stages(u_scr):
        def norm(r):
            rows = slice(r * NORM_ROWS, (r + 1) * NORM_ROWS)
            x = x_ref[rows, :]
            u_scr[rows, :] = (x * _rms_scale(x) * g_ref[...]).astype(BF16)

        return [functools.partial(norm, r) for r in range(x_ref.shape[0] // NORM_ROWS)]

    def project_stages(u_scr, p_scr, g_scr):
        def gate_cols():
            g_scr[...] = _dot_t(u_scr[...], wgt_ref[...])

        def main_cols(c):
            cols = slice(c * tn, (c + 1) * tn)
            p_scr[:, cols] = _dot_t(u_scr[...], wt_ref[cols, :]).astype(BF16)

        return [gate_cols] + [functools.partial(main_cols, c) for c in range(n_main // tn)]

    def mix_stages(p_scr, g_scr):
        def cast_stage():
            w1b_ref[...] = w1_ref[...].astype(BF16)
            w2b_ref[...] = w2_ref[...].astype(BF16)

        return [cast_stage] + _mix_stages(p_scr, g_scr, convw_ref, bias_ref, ghead_ref, y_ref,
                                          c1_scr, c2_scr, ct_scr, m_scr, cw=cw, hd=hd)

    def run_interleaved(*streams):
        merged = []
        for st in streams:
            merged = _merge_evenly(merged, st)
        for stage in merged:
            stage()

    def zero(*refs):
        for ref in refs:
            ref[...] = jnp.zeros_like(ref)

    @pl.when(s == 0)
    def _():
        zero(u_a, u_b, p_a, p_b, g_a, g_b, c1_scr, c2_scr, ct_scr, m_scr)

    @pl.when(jnp.logical_and(s >= 2, lax.rem(s - 2, nc) == 0))
    def _():
        zero(c1_scr, c2_scr, ct_scr, m_scr)

    even = lax.rem(s, 2) == 0

    @pl.when(even)
    def _():
        run_interleaved(project_stages(u_b, p_b, g_b), mix_stages(p_a, g_a), norm_stages(u_a))

    @pl.when(jnp.logical_not(even))
    def _():
        run_interleaved(project_stages(u_a, p_a, g_a), mix_stages(p_b, g_b), norm_stages(u_b))


def _front(x2, g, w_t, w_gate_t, conv_w, bias, g_head, w1, w2, *, n_main, chunk, seq, tn, cw, hd):
    m, d = x2.shape
    mw = N_HEADS * hd
    n_chunks = m // chunk
    nc = seq // chunk
    assert n_chunks % 2 == 0 and n_main % tn == 0 and n_main <= w_t.shape[0]
    r1, r2 = w1.shape[0] // n_chunks, w2.shape[0] // n_chunks
    assert r1 * n_chunks == w1.shape[0] and r2 * n_chunks == w2.shape[0] and r1 % 16 == 0 and r2 % 16 == 0
    norm_blk = lambda s: (jnp.minimum(s, n_chunks - 1), 0)
    mix_blk = lambda s: (jnp.clip(s - 2, 0, n_chunks - 1), 0)
    whole = lambda s: (0, 0)
    kern = functools.partial(_front_kernel, nc=nc, n_main=n_main, tn=tn, cw=cw, hd=hd)
    return pl.pallas_call(
        kern,
        name="front",
        grid=(n_chunks + 2,),
        in_specs=[
            pl.BlockSpec((chunk, d), norm_blk),
            pl.BlockSpec((1, d), whole),
            pl.BlockSpec(w_t.shape, whole, pipeline_mode=pl.Buffered(1)),
            pl.BlockSpec((LANES, d), whole, pipeline_mode=pl.Buffered(1)),
            pl.BlockSpec((CONV_K, cw), whole),
            pl.BlockSpec((1, LANES), whole),
            pl.BlockSpec((1, mw), whole),
            pl.BlockSpec((r1, w1.shape[1]), mix_blk),
            pl.BlockSpec((r2, w2.shape[1]), mix_blk),
        ],
        out_specs=[
            pl.BlockSpec((chunk, cw + mw), mix_blk),
            pl.BlockSpec((r1, w1.shape[1]), mix_blk),
            pl.BlockSpec((r2, w2.shape[1]), mix_blk),
        ],
        out_shape=[
            jax.ShapeDtypeStruct((m, cw + mw), BF16),
            jax.ShapeDtypeStruct(w1.shape, BF16),
            jax.ShapeDtypeStruct(w2.shape, BF16),
        ],
        scratch_shapes=[
            pltpu.VMEM((chunk, d), BF16),
            pltpu.VMEM((chunk, d), BF16),
            pltpu.VMEM((chunk, n_main), BF16),
            pltpu.VMEM((chunk, n_main), BF16),
            pltpu.VMEM((chunk, LANES), F32),
            pltpu.VMEM((chunk, LANES), F32),
            pltpu.VMEM((8, cw), F32),
            pltpu.VMEM((8, cw), F32),
            pltpu.VMEM((N_HEADS, hd + 16, hd), F32),
            pltpu.VMEM((8, LANES), F32),
        ],
        compiler_params=pltpu.CompilerParams(
            dimension_semantics=("arbitrary",),
            vmem_limit_bytes=60 * MIB),
    )(x2, g, w_t, w_gate_t, conv_w, bias, g_head, w1, w2)


def _out_proj_kernel(y_ref, w_ref, x_ref, g_ref, o_ref, wb_scr):
    @pl.when(pl.program_id(0) == 0)
    def _():
        wb_scr[...] = w_ref[...].astype(BF16)

    t = jnp.dot(y_ref[...], wb_scr[...], preferred_element_type=F32)
    o_ref[...] = x_ref[...] + t * _rms_scale(t) * g_ref[...]


def _out_proj(y, w_out, x2, g, *, tm):
    m, d = x2.shape
    e = y.shape[1]
    return pl.pallas_call(
        _out_proj_kernel,
        name="out_proj",
        grid=(m // tm,),
        in_specs=[
            pl.BlockSpec((tm, e), lambda i: (i, 0)),
            pl.BlockSpec((e, d), lambda i: (0, 0), pipeline_mode=pl.Buffered(1)),
            pl.BlockSpec((tm, d), lambda i: (i, 0)),
            pl.BlockSpec((1, d), lambda i: (0, 0)),
        ],
        out_specs=pl.BlockSpec((tm, d), lambda i: (i, 0)),
        out_shape=jax.ShapeDtypeStruct((m, d), F32),
        scratch_shapes=[pltpu.VMEM((e, d), BF16)],
        compiler_params=pltpu.CompilerParams(
            dimension_semantics=("arbitrary",),
            vmem_limit_bytes=56 * MIB),
    )(y, w_out, x2, g)


def _mlp_kernel(x_ref, g1_ref, w1_ref, w2_ref, g2_ref, o_ref, z_scr, *, sub, mid_sub):
    f = pl.program_id(1)
    last = pl.num_programs(1) - 1
    n_sub = x_ref.shape[0] // sub

    def ffn(z):
        h = jnp.dot(z, w1_ref[...], preferred_element_type=F32)
        h = jnp.square(jnp.maximum(h, 0.0)).astype(BF16)
        return jnp.dot(h, w2_ref[...], preferred_element_type=F32)

    @pl.when(f == 0)
    def _():
        for r in range(n_sub):
            rows = slice(r * sub, (r + 1) * sub)
            x = x_ref[rows, :]
            z = (x * _rms_scale(x) * g1_ref[...]).astype(BF16)
            z_scr[rows, :] = z
            o_ref[rows, :] = ffn(z)

    @pl.when(jnp.logical_and(f > 0, f < last))
    def _():
        for r in range(x_ref.shape[0] // mid_sub):
            rows = slice(r * mid_sub, (r + 1) * mid_sub)
            o_ref[rows, :] += ffn(z_scr[rows, :])

    @pl.when(f == last)
    def _():
        for r in range(n_sub):
            rows = slice(r * sub, (r + 1) * sub)
            t = o_ref[rows, :] + ffn(z_scr[rows, :])
            o_ref[rows, :] = x_ref[rows, :] + t * _rms_scale(t) * g2_ref[...]


def _mlp(x1, g1, w1, w2, g2, *, tm, tf, sub, mid_sub, vmem_limit):
    m, d = x1.shape
    dff = w1.shape[1]
    assert dff // tf >= 2
    return pl.pallas_call(
        functools.partial(_mlp_kernel, sub=sub, mid_sub=mid_sub),
        name="mlp",
        grid=(m // tm, dff // tf),
        in_specs=[
            pl.BlockSpec((tm, d), lambda i, f: (i, 0)),
            pl.BlockSpec((1, d), lambda i, f: (0, 0)),
            pl.BlockSpec((d, tf), lambda i, f: (0, f)),
            pl.BlockSpec((tf, d), lambda i, f: (f, 0)),
            pl.BlockSpec((1, d), lambda i, f: (0, 0)),
        ],
        out_specs=pl.BlockSpec((tm, d), lambda i, f: (i, 0)),
        out_shape=jax.ShapeDtypeStruct((m, d), F32),
        scratch_shapes=[pltpu.VMEM((tm, d), BF16)],
        compiler_params=pltpu.CompilerParams(
            dimension_semantics=("arbitrary", "arbitrary"),
            vmem_limit_bytes=vmem_limit),
    )(x1, g1, w1, w2, g2)


def _layer(x2, g_pre_mix, g_post_mix, g_pre_mlp, g_post_mlp, w_in, conv_w, b_i, b_f,
           g_head, w_out, w_mlp1, w_mlp2, *, seq):
    d = x2.shape[1]
    cw = conv_w.shape[1]
    mw = g_head.shape[0]
    hd = mw // N_HEADS
    main = 3 * cw + 4 * mw
    n_gate = 2 * N_HEADS

    w_t = jnp.swapaxes(w_in, 0, 1).astype(BF16)
    w_gate_t = jnp.pad(w_t[main:main + n_gate], ((0, LANES - n_gate), (0, 0)))
    bias = jnp.pad(jnp.concatenate([b_i, b_f]).astype(F32), (0, LANES - n_gate)).reshape(1, LANES)

    y, w1b, w2b = _front(x2, g_pre_mix.reshape(1, d), w_t, w_gate_t, conv_w.astype(F32), bias,
                         g_head.reshape(1, mw).astype(F32), w_mlp1, w_mlp2,
                         n_main=main, chunk=256, seq=seq, tn=1024, cw=cw, hd=hd)
    x1 = _out_proj(y, w_out, x2, g_post_mix.reshape(1, d), tm=512)
    return _mlp(x1, g_pre_mlp.reshape(1, d), w1b, w2b, g_post_mlp.reshape(1, d),
                tm=1024, tf=1024, sub=256, mid_sub=512, vmem_limit=60 * MIB)


def kernel(x, g_pre_mix, g_post_mix, g_pre_mlp, g_post_mlp, w_in, conv_w, b_i, b_f, g_head,
           w_out, w_mlp1, w_mlp2):
    batch, seq, d = x.shape
    x2 = x.reshape(batch * seq, d)
    for layer in range(w_in.shape[0]):
        x2 = _layer(x2, g_pre_mix[layer], g_post_mix[layer], g_pre_mlp[layer], g_post_mlp[layer],
                    w_in[layer], conv_w[layer], b_i[layer], b_f[layer], g_head[layer],
                    w_out[layer], w_mlp1[layer], w_mlp2[layer], seq=seq)
    return x2.reshape(batch, seq, d)
```

```python
import functools

import jax
import jax.numpy as jnp
from jax import lax
from jax.experimental import pallas as pl
from jax.experimental.pallas import tpu as pltpu

EPS = 1e-6
CONV_K = 3
N_HEADS = 4
LANES = 128
NORM_ROWS = 16
CONV_ROWS = 32

F32 = jnp.float32
BF16 = jnp.bfloat16

MIB = 1024 * 1024


def _rms_scale(t):
    return lax.rsqrt(jnp.mean(t * t, axis=-1, keepdims=True) + EPS)


def _dot_t(a, bt):
    return lax.dot_general(a, bt, (((1,), (1,)), ((), ())), preferred_element_type=F32)


def _merge_evenly(a, b):
    out, ia, ib = [], 0, 0
    while ia < len(a) or ib < len(b):
        if ib >= len(b) or (ia < len(a) and ia * len(b) <= ib * len(a)):
            out.append(a[ia])
            ia += 1
        else:
            out.append(b[ib])
            ib += 1
    return out


def _split3(a):
    hi = a.astype(BF16)
    r1 = a - hi.astype(F32)
    mid = r1.astype(BF16)
    lo = (r1 - mid.astype(F32)).astype(BF16)
    return hi, mid, lo


def _mix_stages(p_ref, gate_ref, convw_ref, bias_ref, ghead_ref, y_ref,
                cv_scr, ct_scr, m_scr, done, *, cw, hd):
    shared = {}

    def conv_stage(c):
        done(_mix_conv(c, p_ref, convw_ref, y_ref, cv_scr, cw=cw))

    def gate_stage():
        shared.update(_mix_gates(gate_ref, bias_ref, m_scr))

    def head_stage(h):
        done(_mix_head(h, p_ref, ghead_ref, y_ref, ct_scr, shared, cw=cw, hd=hd))

    convs = [functools.partial(conv_stage, c) for c in range(cw // LANES)]
    heads = [functools.partial(head_stage, h) for h in range(N_HEADS)]
    return [gate_stage] + _merge_evenly(heads, convs)


def _zero_after(tile):
    bits = lax.bitcast_convert_type(tile, jnp.uint32)
    bits = lax.shift_right_logical(lax.shift_right_logical(bits, jnp.uint32(16)), jnp.uint32(16))
    return lax.bitcast_convert_type(bits, F32)


def _mix_conv(c, p_ref, convw_ref, y_ref, cv_scr, *, cw):
    L = p_ref.shape[0]
    cols = slice(c * LANES, (c + 1) * LANES)
    w = convw_ref[:, cols]
    prev = cv_scr[:, cols]
    seen = jnp.zeros((8, LANES), F32)
    for r in range(L // CONV_ROWS):
        rows = slice(r * CONV_ROWS, (r + 1) * CONV_ROWS)
        cx = p_ref[rows, c * LANES:(c + 1) * LANES].astype(F32)
        cb = p_ref[rows, cw + c * LANES:cw + (c + 1) * LANES].astype(F32)
        cc = p_ref[rows, 2 * cw + c * LANES:2 * cw + (c + 1) * LANES].astype(F32)
        u = cc * cx
        ext = jnp.concatenate([prev, u], axis=0)
        s1 = pltpu.roll(ext, 1, 0)[8:]
        s2 = pltpu.roll(ext, 2, 0)[8:]
        y_conv = cb * (w[0:1, :] * s2 + w[1:2, :] * s1 + w[2:3, :] * u)
        y_ref[rows, cols] = y_conv.astype(y_ref.dtype)
        prev = u[CONV_ROWS - 8:]
        seen = seen + y_conv[0:8]
    cv_scr[:, cols] = prev
    return seen


def _mix_gates(gate_ref, bias_ref, m_scr):
    L = gate_ref.shape[0]
    a = gate_ref[...] + bias_ref[...]
    lane = lax.broadcasted_iota(jnp.int32, a.shape, 1)
    gl = jnp.where(lane < N_HEADS, a, jax.nn.log_sigmoid(a))
    i0 = lax.broadcasted_iota(jnp.int32, (L, L), 0)
    i1 = lax.broadcasted_iota(jnp.int32, (L, L), 1)
    tril = (i0 >= i1).astype(BF16)
    hi, mid, lo = _split3(gl)
    bc = (jnp.dot(tril, hi, preferred_element_type=F32)
          + jnp.dot(tril, mid, preferred_element_type=F32)
          + jnp.dot(tril, lo, preferred_element_type=F32))
    bsh = pltpu.roll(bc, LANES - N_HEADS, 1)
    ab = jnp.where(lane < N_HEADS, gl - bsh, 0.0)
    b_rows = bsh.T
    b_last = bsh[L - 1:L, :]
    m_prev_v = m_scr[0:1, :]
    m_new_v = b_last + jnp.maximum(m_prev_v, jnp.max(ab, axis=0, keepdims=True))
    decay_v = jnp.exp(b_last + m_prev_v - m_new_v)
    wgt = jnp.exp(ab + (b_last - m_new_v))
    m_scr[...] = jnp.broadcast_to(m_new_v, m_scr.shape)
    return dict(ab=ab, b_rows=b_rows, m_prev_v=m_prev_v, decay_v=decay_v, wgt=wgt,
                keep=i0 <= i1)


def _mix_head(h, p_ref, ghead_ref, y_ref, ct_scr, shared, *, cw, hd):
    mw = N_HEADS * hd
    q0, k0, v0, o0 = 3 * cw, 3 * cw + mw, 3 * cw + 2 * mw, 3 * cw + 3 * mw
    cs = slice(h * hd, (h + 1) * hd)
    q = p_ref[:, q0 + h * hd:q0 + (h + 1) * hd]
    k = p_ref[:, k0 + h * hd:k0 + (h + 1) * hd]
    v = p_ref[:, v0 + h * hd:v0 + (h + 1) * hd]
    o = p_ref[:, o0 + h * hd:o0 + (h + 1) * hd].astype(F32)

    b_row = shared["b_rows"][h:h + 1, :]
    a_col = shared["ab"][:, h:h + 1]
    m_prev = shared["m_prev_v"][:, h:h + 1]
    ctn = ct_scr[h]

    m_inter = b_row + m_prev
    log_d = jnp.where(shared["keep"], a_col + b_row, -jnp.inf)
    m_row = jnp.maximum(m_inter, jnp.max(log_d, axis=0, keepdims=True))
    scores = _dot_t(k, q) * jnp.exp(log_d - m_row)
    inter = jnp.exp(m_inter - m_row)
    qc = _dot_t(ctn.astype(BF16), q)
    v_t = v.astype(F32).T.astype(BF16)
    num = jnp.dot(v_t, scores.astype(BF16), preferred_element_type=F32) + inter * qc[0:hd]
    den = jnp.sum(scores, axis=0, keepdims=True) + inter * qc[hd:hd + 1]
    floor = (hd ** 0.5) * jnp.exp(-m_row)
    h_t = num * (1.0 / jnp.maximum(jnp.abs(den), floor))
    hn_t = h_t * lax.rsqrt(jnp.mean(h_t * h_t, axis=0, keepdims=True) + EPS)
    y_h = jax.nn.sigmoid(o) * (hn_t.T * ghead_ref[:, cs])
    y_ref[:, cw + h * hd:cw + (h + 1) * hd] = y_h.astype(y_ref.dtype)

    decay = shared["decay_v"][:, h:h + 1]
    kw = k.astype(F32) * shared["wgt"][:, h:h + 1]
    c_new = decay * ctn[0:hd] + jnp.dot(v_t, kw.astype(BF16), preferred_element_type=F32)
    ct_scr[h, 0:hd, :] = c_new
    ct_scr[h, hd:hd + 1, :] = decay * ctn[hd:hd + 1] + jnp.sum(kw, axis=0, keepdims=True)
    return y_h[0:8, 0:LANES] + c_new[0:8, 0:LANES]


def _front_kernel(x_ref, g_ref, wt_ref, wgt_ref, convw_ref, bias_ref, ghead_ref, w1_ref, w2_ref,
                  y_ref, w1b_ref, w2b_ref,
                  u_a, u_b, p_a, p_b, g_a, g_b, cv_scr, ct_scr, m_scr,
                  *, nc, n_main, tn, cw, hd):
    s = pl.program_id(0)

    pending = []

    def done(tile):
        pending.append(tile)

    def anchored(res):
        if not pending:
            return res
        tile = functools.reduce(lambda a, b: a + b, pending)
        pending.clear()
        z = jnp.concatenate([_zero_after(tile)] * (res.shape[1] // LANES), axis=1)
        return jnp.concatenate([res[0:8] + z, res[8:]], axis=0)

    def norm_stages(u_scr):
        def norm(r):
            rows = slice(r * NORM_ROWS, (r + 1) * NORM_ROWS)
            x = x_ref[rows, :]
            u = x * _rms_scale(x) * g_ref[...]
            u_scr[rows, :] = u.astype(BF16)
            done(u[0:8, 0:LANES])

        return [functools.partial(norm, r) for r in range(x_ref.shape[0] // NORM_ROWS)]

    def project_stages(u_scr, p_scr, g_scr):
        def gate_cols():
            g_scr[...] = _dot_t(u_scr[...], wgt_ref[...])

        def main_cols(c):
            cols = slice(c * tn, (c + 1) * tn)
            p_scr[:, cols] = anchored(_dot_t(u_scr[...], wt_ref[cols, :])).astype(BF16)

        return [gate_cols] + [functools.partial(main_cols, c) for c in range(n_main // tn)]

    def mix_stages(p_scr, g_scr):
        def cast_stage():
            w1b_ref[...] = w1_ref[...].astype(BF16)
            w2b_ref[...] = w2_ref[...].astype(BF16)

        return [cast_stage] + _mix_stages(p_scr, g_scr, convw_ref, bias_ref, ghead_ref, y_ref,
                                          cv_scr, ct_scr, m_scr, done, cw=cw, hd=hd)

    def run_interleaved(*streams):
        *vector_streams, matmul_stream = streams
        merged = []
        for st in vector_streams:
            merged = _merge_evenly(merged, st)
        n_tail = len(matmul_stream) // 8 if merged else 0
        split = len(matmul_stream) - n_tail
        merged = _merge_evenly(matmul_stream[:split], merged) + matmul_stream[split:]
        pending.clear()
        for stage in merged:
            stage()
        pending.clear()

    def zero(*refs):
        for ref in refs:
            ref[...] = jnp.zeros_like(ref)

    @pl.when(s == 0)
    def _():
        zero(u_a, u_b, p_a, p_b, g_a, g_b, cv_scr, ct_scr, m_scr)

    @pl.when(jnp.logical_and(s >= 2, lax.rem(s - 2, nc) == 0))
    def _():
        zero(cv_scr, ct_scr, m_scr)

    even = lax.rem(s, 2) == 0

    @pl.when(even)
    def _():
        run_interleaved(mix_stages(p_a, g_a), norm_stages(u_a), project_stages(u_b, p_b, g_b))

    @pl.when(jnp.logical_not(even))
    def _():
        run_interleaved(mix_stages(p_b, g_b), norm_stages(u_b), project_stages(u_a, p_a, g_a))


def _front(x2, g, w_t, w_gate_t, conv_w, bias, g_head, w1, w2, *, n_main, chunk, seq, tn, cw, hd):
    m, d = x2.shape
    mw = N_HEADS * hd
    n_chunks = m // chunk
    nc = seq // chunk
    assert n_chunks % 2 == 0 and n_main % tn == 0 and n_main <= w_t.shape[0]
    r1, r2 = w1.shape[0] // n_chunks, w2.shape[0] // n_chunks
    assert r1 * n_chunks == w1.shape[0] and r2 * n_chunks == w2.shape[0] and r1 % 16 == 0 and r2 % 16 == 0
    norm_blk = lambda s: (jnp.minimum(s, n_chunks - 1), 0)
    mix_blk = lambda s: (jnp.clip(s - 2, 0, n_chunks - 1), 0)
    whole = lambda s: (0, 0)
    kern = functools.partial(_front_kernel, nc=nc, n_main=n_main, tn=tn, cw=cw, hd=hd)
    return pl.pallas_call(
        kern,
        name="front",
        grid=(n_chunks + 2,),
        in_specs=[
            pl.BlockSpec((chunk, d), norm_blk),
            pl.BlockSpec((1, d), whole),
            pl.BlockSpec(w_t.shape, whole, pipeline_mode=pl.Buffered(1)),
            pl.BlockSpec((LANES, d), whole, pipeline_mode=pl.Buffered(1)),
            pl.BlockSpec((CONV_K, cw), whole),
            pl.BlockSpec((1, LANES), whole),
            pl.BlockSpec((1, mw), whole),
            pl.BlockSpec((r1, w1.shape[1]), mix_blk),
            pl.BlockSpec((r2, w2.shape[1]), mix_blk),
        ],
        out_specs=[
            pl.BlockSpec((chunk, cw + mw), mix_blk),
            pl.BlockSpec((r1, w1.shape[1]), mix_blk),
            pl.BlockSpec((r2, w2.shape[1]), mix_blk),
        ],
        out_shape=[
            jax.ShapeDtypeStruct((m, cw + mw), BF16),
            jax.ShapeDtypeStruct(w1.shape, BF16),
            jax.ShapeDtypeStruct(w2.shape, BF16),
        ],
        scratch_shapes=[
            pltpu.VMEM((chunk, d), BF16),
            pltpu.VMEM((chunk, d), BF16),
            pltpu.VMEM((chunk, n_main), BF16),
            pltpu.VMEM((chunk, n_main), BF16),
            pltpu.VMEM((chunk, LANES), F32),
            pltpu.VMEM((chunk, LANES), F32),
            pltpu.VMEM((8, cw), F32),
            pltpu.VMEM((N_HEADS, hd + 16, hd), F32),
            pltpu.VMEM((8, LANES), F32),
        ],
        compiler_params=pltpu.CompilerParams(
            dimension_semantics=("arbitrary",),
            vmem_limit_bytes=60 * MIB),
    )(x2, g, w_t, w_gate_t, conv_w, bias, g_head, w1, w2)


def _out_proj_kernel(y_ref, w_ref, x_ref, g_ref, o_ref, wb_scr):
    @pl.when(pl.program_id(0) == 0)
    def _():
        wb_scr[...] = w_ref[...].astype(BF16)

    t = jnp.dot(y_ref[...], wb_scr[...], preferred_element_type=F32)
    o_ref[...] = x_ref[...] + t * _rms_scale(t) * g_ref[...]


def _out_proj(y, w_out, x2, g, *, tm):
    m, d = x2.shape
    e = y.shape[1]
    return pl.pallas_call(
        _out_proj_kernel,
        name="out_proj",
        grid=(m // tm,),
        in_specs=[
            pl.BlockSpec((tm, e), lambda i: (i, 0)),
            pl.BlockSpec((e, d), lambda i: (0, 0), pipeline_mode=pl.Buffered(1)),
            pl.BlockSpec((tm, d), lambda i: (i, 0)),
            pl.BlockSpec((1, d), lambda i: (0, 0)),
        ],
        out_specs=pl.BlockSpec((tm, d), lambda i: (i, 0)),
        out_shape=jax.ShapeDtypeStruct((m, d), F32),
        scratch_shapes=[pltpu.VMEM((e, d), BF16)],
        compiler_params=pltpu.CompilerParams(
            dimension_semantics=("arbitrary",),
            vmem_limit_bytes=56 * MIB),
    )(y, w_out, x2, g)


def _mlp_kernel(x_ref, g1_ref, w1_ref, w2_ref, g2_ref, o_ref, z_scr, *, sub, mid_sub):
    f = pl.program_id(1)
    last = pl.num_programs(1) - 1
    n_sub = x_ref.shape[0] // sub

    def ffn(z):
        h = jnp.dot(z, w1_ref[...], preferred_element_type=F32)
        h = jnp.square(jnp.maximum(h, 0.0)).astype(BF16)
        return jnp.dot(h, w2_ref[...], preferred_element_type=F32)

    @pl.when(f == 0)
    def _():
        for r in range(n_sub):
            rows = slice(r * sub, (r + 1) * sub)
            x = x_ref[rows, :]
            z = (x * _rms_scale(x) * g1_ref[...]).astype(BF16)
            z_scr[rows, :] = z
            o_ref[rows, :] = ffn(z)

    @pl.when(jnp.logical_and(f > 0, f < last))
    def _():
        for r in range(x_ref.shape[0] // mid_sub):
            rows = slice(r * mid_sub, (r + 1) * mid_sub)
            o_ref[rows, :] += ffn(z_scr[rows, :])

    @pl.when(f == last)
    def _():
        for r in range(n_sub):
            rows = slice(r * sub, (r + 1) * sub)
            t = o_ref[rows, :] + ffn(z_scr[rows, :])
            o_ref[rows, :] = x_ref[rows, :] + t * _rms_scale(t) * g2_ref[...]


def _mlp(x1, g1, w1, w2, g2, *, tm, tf, sub, mid_sub, vmem_limit):
    m, d = x1.shape
    dff = w1.shape[1]
    assert dff // tf >= 2
    return pl.pallas_call(
        functools.partial(_mlp_kernel, sub=sub, mid_sub=mid_sub),
        name="mlp",
        grid=(m // tm, dff // tf),
        in_specs=[
            pl.BlockSpec((tm, d), lambda i, f: (i, 0)),
            pl.BlockSpec((1, d), lambda i, f: (0, 0)),
            pl.BlockSpec((d, tf), lambda i, f: (0, f)),
            pl.BlockSpec((tf, d), lambda i, f: (f, 0)),
            pl.BlockSpec((1, d), lambda i, f: (0, 0)),
        ],
        out_specs=pl.BlockSpec((tm, d), lambda i, f: (i, 0)),
        out_shape=jax.ShapeDtypeStruct((m, d), F32),
        scratch_shapes=[pltpu.VMEM((tm, d), BF16)],
        compiler_params=pltpu.CompilerParams(
            dimension_semantics=("arbitrary", "arbitrary"),
            vmem_limit_bytes=vmem_limit),
    )(x1, g1, w1, w2, g2)


def _layer(x2, g_pre_mix, g_post_mix, g_pre_mlp, g_post_mlp, w_in, conv_w, b_i, b_f,
           g_head, w_out, w_mlp1, w_mlp2, *, seq):
    d = x2.shape[1]
    cw = conv_w.shape[1]
    mw = g_head.shape[0]
    hd = mw // N_HEADS
    main = 3 * cw + 4 * mw
    n_gate = 2 * N_HEADS

    w_t = jnp.swapaxes(w_in, 0, 1).astype(BF16)
    w_gate_t = jnp.pad(w_t[main:main + n_gate], ((0, LANES - n_gate), (0, 0)))
    bias = jnp.pad(jnp.concatenate([b_i, b_f]).astype(F32), (0, LANES - n_gate)).reshape(1, LANES)

    y, w1b, w2b = _front(x2, g_pre_mix.reshape(1, d), w_t, w_gate_t, conv_w.astype(F32), bias,
                         g_head.reshape(1, mw).astype(F32), w_mlp1, w_mlp2,
                         n_main=main, chunk=256, seq=seq, tn=256, cw=cw, hd=hd)
    x1 = _out_proj(y, w_out, x2, g_post_mix.reshape(1, d), tm=512)
    return _mlp(x1, g_pre_mlp.reshape(1, d), w1b, w2b, g_post_mlp.reshape(1, d),
                tm=1024, tf=1024, sub=256, mid_sub=512, vmem_limit=60 * MIB)


def kernel(x, g_pre_mix, g_post_mix, g_pre_mlp, g_post_mlp, w_in, conv_w, b_i, b_f, g_head,
           w_out, w_mlp1, w_mlp2):
    batch, seq, d = x.shape
    x2 = x.reshape(batch * seq, d)
    for layer in range(w_in.shape[0]):
        x2 = _layer(x2, g_pre_mix[layer], g_post_mix[layer], g_pre_mlp[layer], g_post_mlp[layer],
                    w_in[layer], conv_w[layer], b_i[layer], b_f[layer], g_head[layer],
                    w_out[layer], w_mlp1[layer], w_mlp2[layer], seq=seq)
    return x2.reshape(batch, seq, d)
```

```python
import functools

import jax
import jax.numpy as jnp
from jax import lax
from jax.experimental import pallas as pl
from jax.experimental.pallas import tpu as pltpu

EPS = 1e-6
CONV_K = 3
N_HEADS = 4
LANES = 128
NORM_ROWS = 16
CONV_ROWS = 32

F32 = jnp.float32
BF16 = jnp.bfloat16

MIB = 1024 * 1024


def _rms_scale(t):
    return lax.rsqrt(jnp.mean(t * t, axis=-1, keepdims=True) + EPS)


def _dot_t(a, bt):
    return lax.dot_general(a, bt, (((1,), (1,)), ((), ())), preferred_element_type=F32)


def _merge_evenly(a, b):
    wa, wb = sum(c for c, _ in a), sum(c for c, _ in b)
    out, ia, ib, ca, cb = [], 0, 0, 0.0, 0.0
    while ia < len(a) or ib < len(b):
        if ib >= len(b) or (ia < len(a) and ca * wb <= cb * wa):
            out.append(a[ia])
            ca += a[ia][0]
            ia += 1
        else:
            out.append(b[ib])
            cb += b[ib][0]
            ib += 1
    return out


def _split3(a):
    hi = a.astype(BF16)
    r1 = a - hi.astype(F32)
    mid = r1.astype(BF16)
    lo = (r1 - mid.astype(F32)).astype(BF16)
    return hi, mid, lo


def _mix_stages(p_ref, gate_ref, convw_ref, bias_ref, ghead_ref, y_ref,
                cv_scr, ct_scr, m_scr, done, *, cw, hd):
    shared = {}

    def conv_stage(c):
        done(_mix_conv(c, p_ref, convw_ref, y_ref, cv_scr, cw=cw))

    def gate_stage():
        shared.update(_mix_gates(gate_ref, bias_ref, m_scr))

    def head_stages(h):
        local = {}

        def scores():
            local.update(_head_scores(h, p_ref, ct_scr, shared, cw=cw, hd=hd))
            done(local["scores"][0:8, 0:LANES])

        def output():
            done(_head_output(h, p_ref, ghead_ref, y_ref, local, cw=cw, hd=hd))

        def state():
            done(_head_state(h, p_ref, ct_scr, shared, local, cw=cw, hd=hd))

        return [(2.0, scores), (3.0, output), (1.5, state)]

    convs = [(1.0, functools.partial(conv_stage, c)) for c in range(cw // LANES)]
    heads = [st for h in range(N_HEADS) for st in head_stages(h)]
    return [(2.0, gate_stage)] + _merge_evenly(heads, convs)


def _zero_after(tile):
    bits = lax.bitcast_convert_type(tile, jnp.uint32)
    bits = lax.shift_right_logical(lax.shift_right_logical(bits, jnp.uint32(16)), jnp.uint32(16))
    return lax.bitcast_convert_type(bits, F32)


def _anchor(res, tile):
    z = jnp.concatenate([_zero_after(tile)] * (res.shape[1] // LANES), axis=1)
    return jnp.concatenate([res[0:8] + z, res[8:]], axis=0)


def _mix_conv(c, p_ref, convw_ref, y_ref, cv_scr, *, cw):
    L = p_ref.shape[0]
    cols = slice(c * LANES, (c + 1) * LANES)
    w = convw_ref[:, cols]
    prev = cv_scr[:, cols]
    seen = jnp.zeros((8, LANES), F32)
    for r in range(L // CONV_ROWS):
        rows = slice(r * CONV_ROWS, (r + 1) * CONV_ROWS)
        cx = p_ref[rows, c * LANES:(c + 1) * LANES].astype(F32)
        cb = p_ref[rows, cw + c * LANES:cw + (c + 1) * LANES].astype(F32)
        cc = p_ref[rows, 2 * cw + c * LANES:2 * cw + (c + 1) * LANES].astype(F32)
        u = cc * cx
        ext = jnp.concatenate([prev, u], axis=0)
        s1 = pltpu.roll(ext, 1, 0)[8:]
        s2 = pltpu.roll(ext, 2, 0)[8:]
        y_conv = cb * (w[0:1, :] * s2 + w[1:2, :] * s1 + w[2:3, :] * u)
        y_ref[rows, cols] = y_conv.astype(y_ref.dtype)
        prev = u[CONV_ROWS - 8:]
        seen = seen + y_conv[0:8]
    cv_scr[:, cols] = prev
    return seen


def _mix_gates(gate_ref, bias_ref, m_scr):
    L = gate_ref.shape[0]
    a = gate_ref[...] + bias_ref[...]
    lane = lax.broadcasted_iota(jnp.int32, a.shape, 1)
    gl = jnp.where(lane < N_HEADS, a, jax.nn.log_sigmoid(a))
    i0 = lax.broadcasted_iota(jnp.int32, (L, L), 0)
    i1 = lax.broadcasted_iota(jnp.int32, (L, L), 1)
    tril = (i0 >= i1).astype(BF16)
    hi, mid, lo = _split3(gl)
    bc = (jnp.dot(tril, hi, preferred_element_type=F32)
          + jnp.dot(tril, mid, preferred_element_type=F32)
          + jnp.dot(tril, lo, preferred_element_type=F32))
    bsh = pltpu.roll(bc, LANES - N_HEADS, 1)
    ab = jnp.where(lane < N_HEADS, gl - bsh, 0.0)
    b_rows = bsh.T
    b_last = bsh[L - 1:L, :]
    m_prev_v = m_scr[0:1, :]
    m_new_v = b_last + jnp.maximum(m_prev_v, jnp.max(ab, axis=0, keepdims=True))
    decay_v = jnp.exp(b_last + m_prev_v - m_new_v)
    wgt = jnp.exp(ab + (b_last - m_new_v))
    m_scr[...] = jnp.broadcast_to(m_new_v, m_scr.shape)
    return dict(ab=ab, b_rows=b_rows, m_prev_v=m_prev_v, decay_v=decay_v, wgt=wgt,
                keep=i0 <= i1)


def _head_cols(h, *, cw, hd):
    mw = N_HEADS * hd
    return tuple(slice(3 * cw + i * mw + h * hd, 3 * cw + i * mw + (h + 1) * hd) for i in range(4))


def _head_scores(h, p_ref, ct_scr, shared, *, cw, hd):
    qs, ks, _, _ = _head_cols(h, cw=cw, hd=hd)
    q = p_ref[:, qs]
    k = p_ref[:, ks]
    b_row = shared["b_rows"][h:h + 1, :]
    a_col = shared["ab"][:, h:h + 1]
    m_prev = shared["m_prev_v"][:, h:h + 1]
    m_inter = b_row + m_prev
    log_d = jnp.where(shared["keep"], a_col + b_row, -jnp.inf)
    m_row = jnp.maximum(m_inter, jnp.max(log_d, axis=0, keepdims=True))
    scores = _dot_t(k, q) * jnp.exp(log_d - m_row)
    return dict(scores=scores, m_row=m_row, inter=jnp.exp(m_inter - m_row), ctn=ct_scr[h])


def _head_output(h, p_ref, ghead_ref, y_ref, local, *, cw, hd):
    qs, _, vs, os_ = _head_cols(h, cw=cw, hd=hd)
    q = p_ref[:, qs]
    o = p_ref[:, os_].astype(F32)
    scores, m_row, inter, ctn = local["scores"], local["m_row"], local["inter"], local["ctn"]
    qc = _dot_t(ctn.astype(BF16), q)
    v_t = p_ref[:, vs].astype(F32).T.astype(BF16)
    local["v_t"] = v_t
    num = jnp.dot(v_t, scores.astype(BF16), preferred_element_type=F32) + inter * qc[0:hd]
    den = jnp.sum(scores, axis=0, keepdims=True) + inter * qc[hd:hd + 1]
    floor = (hd ** 0.5) * jnp.exp(-m_row)
    h_t = num * (1.0 / jnp.maximum(jnp.abs(den), floor))
    hn_t = h_t * lax.rsqrt(jnp.mean(h_t * h_t, axis=0, keepdims=True) + EPS)
    y_h = jax.nn.sigmoid(o) * (hn_t.T * ghead_ref[:, h * hd:(h + 1) * hd])
    y_ref[:, cw + h * hd:cw + (h + 1) * hd] = y_h.astype(y_ref.dtype)
    return y_h[0:8, 0:LANES]


def _head_state(h, p_ref, ct_scr, shared, local, *, cw, hd):
    _, ks, _, _ = _head_cols(h, cw=cw, hd=hd)
    ctn = local["ctn"]
    decay = shared["decay_v"][:, h:h + 1]
    kw = p_ref[:, ks].astype(F32) * shared["wgt"][:, h:h + 1]
    c_new = decay * ctn[0:hd] + jnp.dot(local["v_t"], kw.astype(BF16), preferred_element_type=F32)
    ct_scr[h, 0:hd, :] = c_new
    ct_scr[h, hd:hd + 1, :] = decay * ctn[hd:hd + 1] + jnp.sum(kw, axis=0, keepdims=True)
    return c_new[0:8, 0:LANES]


def _front_kernel(x_ref, g_ref, wt_ref, wgt_ref, convw_ref, bias_ref, ghead_ref, w1_ref, w2_ref,
                  y_ref, w1b_ref, w2b_ref,
                  u_a, u_b, p_a, p_b, g_a, g_b, cv_scr, ct_scr, m_scr,
                  *, nc, n_main, tn, cw, hd):
    s = pl.program_id(0)

    pending = []

    def done(tile):
        pending.append(tile)

    def anchored(res):
        if not pending:
            return res
        tile = functools.reduce(lambda a, b: a + b, pending)
        pending.clear()
        return _anchor(res, tile)

    def norm_stages(u_scr):
        def norm(r):
            rows = slice(r * NORM_ROWS, (r + 1) * NORM_ROWS)
            x = x_ref[rows, :]
            u = x * _rms_scale(x) * g_ref[...]
            u_scr[rows, :] = u.astype(BF16)
            done(u[0:8, 0:LANES])

        return [(0.3, functools.partial(norm, r)) for r in range(x_ref.shape[0] // NORM_ROWS)]

    def project_stages(u_scr, p_scr, g_scr):
        def gate_cols():
            g_scr[...] = _dot_t(u_scr[...], wgt_ref[...])

        def main_cols(c):
            cols = slice(c * tn, (c + 1) * tn)
            p_scr[:, cols] = anchored(_dot_t(u_scr[...], wt_ref[cols, :])).astype(BF16)

        return ([(0.5, gate_cols)]
                + [(1.0, functools.partial(main_cols, c)) for c in range(n_main // tn)])

    def mix_stages(p_scr, g_scr):
        def cast_stage():
            w1b_ref[...] = w1_ref[...].astype(BF16)
            w2b_ref[...] = w2_ref[...].astype(BF16)

        return [(0.5, cast_stage)] + _mix_stages(p_scr, g_scr, convw_ref, bias_ref, ghead_ref, y_ref,
                                          cv_scr, ct_scr, m_scr, done, cw=cw, hd=hd)

    def run_interleaved(*streams):
        *vector_streams, matmul_stream = streams
        merged = []
        for st in vector_streams:
            merged = _merge_evenly(merged, st)
        n_tail = len(matmul_stream) // 8 if merged else 0
        split = len(matmul_stream) - n_tail
        merged = _merge_evenly(matmul_stream[:split], merged) + matmul_stream[split:]
        pending.clear()
        for _, stage in merged:
            stage()
        pending.clear()

    def zero(*refs):
        for ref in refs:
            ref[...] = jnp.zeros_like(ref)

    @pl.when(s == 0)
    def _():
        zero(u_a, u_b, p_a, p_b, g_a, g_b, cv_scr, ct_scr, m_scr)

    @pl.when(jnp.logical_and(s >= 2, lax.rem(s - 2, nc) == 0))
    def _():
        zero(cv_scr, ct_scr, m_scr)

    even = lax.rem(s, 2) == 0

    @pl.when(even)
    def _():
        run_interleaved(mix_stages(p_a, g_a), norm_stages(u_a), project_stages(u_b, p_b, g_b))

    @pl.when(jnp.logical_not(even))
    def _():
        run_interleaved(mix_stages(p_b, g_b), norm_stages(u_b), project_stages(u_a, p_a, g_a))


def _front(x2, g, w_t, w_gate_t, conv_w, bias, g_head, w1, w2, *, n_main, chunk, seq, tn, cw, hd):
    m, d = x2.shape
    mw = N_HEADS * hd
    n_chunks = m // chunk
    nc = seq // chunk
    assert n_chunks % 2 == 0 and n_main % tn == 0 and n_main <= w_t.shape[0]
    r1, r2 = w1.shape[0] // n_chunks, w2.shape[0] // n_chunks
    assert r1 * n_chunks == w1.shape[0] and r2 * n_chunks == w2.shape[0] and r1 % 16 == 0 and r2 % 16 == 0
    norm_blk = lambda s: (jnp.minimum(s, n_chunks - 1), 0)
    mix_blk = lambda s: (jnp.clip(s - 2, 0, n_chunks - 1), 0)
    whole = lambda s: (0, 0)
    kern = functools.partial(_front_kernel, nc=nc, n_main=n_main, tn=tn, cw=cw, hd=hd)
    return pl.pallas_call(
        kern,
        name="front",
        grid=(n_chunks + 2,),
        in_specs=[
            pl.BlockSpec((chunk, d), norm_blk),
            pl.BlockSpec((1, d), whole),
            pl.BlockSpec(w_t.shape, whole, pipeline_mode=pl.Buffered(1)),
            pl.BlockSpec((LANES, d), whole, pipeline_mode=pl.Buffered(1)),
            pl.BlockSpec((CONV_K, cw), whole),
            pl.BlockSpec((1, LANES), whole),
            pl.BlockSpec((1, mw), whole),
            pl.BlockSpec((r1, w1.shape[1]), mix_blk),
            pl.BlockSpec((r2, w2.shape[1]), mix_blk),
        ],
        out_specs=[
            pl.BlockSpec((chunk, cw + mw), mix_blk),
            pl.BlockSpec((r1, w1.shape[1]), mix_blk),
            pl.BlockSpec((r2, w2.shape[1]), mix_blk),
        ],
        out_shape=[
            jax.ShapeDtypeStruct((m, cw + mw), BF16),
            jax.ShapeDtypeStruct(w1.shape, BF16),
            jax.ShapeDtypeStruct(w2.shape, BF16),
        ],
        scratch_shapes=[
            pltpu.VMEM((chunk, d), BF16),
            pltpu.VMEM((chunk, d), BF16),
            pltpu.VMEM((chunk, n_main), BF16),
            pltpu.VMEM((chunk, n_main), BF16),
            pltpu.VMEM((chunk, LANES), F32),
            pltpu.VMEM((chunk, LANES), F32),
            pltpu.VMEM((8, cw), F32),
            pltpu.VMEM((N_HEADS, hd + 16, hd), F32),
            pltpu.VMEM((8, LANES), F32),
        ],
        compiler_params=pltpu.CompilerParams(
            dimension_semantics=("arbitrary",),
            vmem_limit_bytes=60 * MIB),
    )(x2, g, w_t, w_gate_t, conv_w, bias, g_head, w1, w2)


def _out_proj_kernel(y_ref, w_ref, x_ref, g_ref, o_ref, wb_scr):
    @pl.when(pl.program_id(0) == 0)
    def _():
        wb_scr[...] = w_ref[...].astype(BF16)

    t = jnp.dot(y_ref[...], wb_scr[...], preferred_element_type=F32)
    o_ref[...] = x_ref[...] + t * _rms_scale(t) * g_ref[...]


def _out_proj(y, w_out, x2, g, *, tm):
    m, d = x2.shape
    e = y.shape[1]
    return pl.pallas_call(
        _out_proj_kernel,
        name="out_proj",
        grid=(m // tm,),
        in_specs=[
            pl.BlockSpec((tm, e), lambda i: (i, 0)),
            pl.BlockSpec((e, d), lambda i: (0, 0), pipeline_mode=pl.Buffered(1)),
            pl.BlockSpec((tm, d), lambda i: (i, 0)),
            pl.BlockSpec((1, d), lambda i: (0, 0)),
        ],
        out_specs=pl.BlockSpec((tm, d), lambda i: (i, 0)),
        out_shape=jax.ShapeDtypeStruct((m, d), F32),
        scratch_shapes=[pltpu.VMEM((e, d), BF16)],
        compiler_params=pltpu.CompilerParams(
            dimension_semantics=("arbitrary",),
            vmem_limit_bytes=56 * MIB),
    )(y, w_out, x2, g)


def _mlp_kernel(x_ref, g1_ref, w1_ref, w2_ref, g2_ref, o_ref, z_scr, *, sub, mid_sub):
    f = pl.program_id(1)
    last = pl.num_programs(1) - 1
    n_sub = x_ref.shape[0] // sub

    def ffn(z):
        h = jnp.dot(z, w1_ref[...], preferred_element_type=F32)
        h = jnp.square(jnp.maximum(h, 0.0)).astype(BF16)
        return jnp.dot(h, w2_ref[...], preferred_element_type=F32)

    @pl.when(f == 0)
    def _():
        for r in range(n_sub):
            rows = slice(r * sub, (r + 1) * sub)
            x = x_ref[rows, :]
            z = (x * _rms_scale(x) * g1_ref[...]).astype(BF16)
            z_scr[rows, :] = z
            o_ref[rows, :] = ffn(z)

    @pl.when(jnp.logical_and(f > 0, f < last))
    def _():
        for r in range(x_ref.shape[0] // mid_sub):
            rows = slice(r * mid_sub, (r + 1) * mid_sub)
            o_ref[rows, :] += ffn(z_scr[rows, :])

    @pl.when(f == last)
    def _():
        for r in range(n_sub):
            rows = slice(r * sub, (r + 1) * sub)
            t = o_ref[rows, :] + ffn(z_scr[rows, :])
            o_ref[rows, :] = x_ref[rows, :] + t * _rms_scale(t) * g2_ref[...]


def _mlp(x1, g1, w1, w2, g2, *, tm, tf, sub, mid_sub, vmem_limit):
    m, d = x1.shape
    dff = w1.shape[1]
    assert dff // tf >= 2
    return pl.pallas_call(
        functools.partial(_mlp_kernel, sub=sub, mid_sub=mid_sub),
        name="mlp",
        grid=(m // tm, dff // tf),
        in_specs=[
            pl.BlockSpec((tm, d), lambda i, f: (i, 0)),
            pl.BlockSpec((1, d), lambda i, f: (0, 0)),
            pl.BlockSpec((d, tf), lambda i, f: (0, f)),
            pl.BlockSpec((tf, d), lambda i, f: (f, 0)),
            pl.BlockSpec((1, d), lambda i, f: (0, 0)),
        ],
        out_specs=pl.BlockSpec((tm, d), lambda i, f: (i, 0)),
        out_shape=jax.ShapeDtypeStruct((m, d), F32),
        scratch_shapes=[pltpu.VMEM((tm, d), BF16)],
        compiler_params=pltpu.CompilerParams(
            dimension_semantics=("arbitrary", "arbitrary"),
            vmem_limit_bytes=vmem_limit),
    )(x1, g1, w1, w2, g2)


def _layer(x2, g_pre_mix, g_post_mix, g_pre_mlp, g_post_mlp, w_in, conv_w, b_i, b_f,
           g_head, w_out, w_mlp1, w_mlp2, *, seq):
    d = x2.shape[1]
    cw = conv_w.shape[1]
    mw = g_head.shape[0]
    hd = mw // N_HEADS
    main = 3 * cw + 4 * mw
    n_gate = 2 * N_HEADS

    w_t = jnp.swapaxes(w_in, 0, 1).astype(BF16)
    w_gate_t = jnp.pad(w_t[main:main + n_gate], ((0, LANES - n_gate), (0, 0)))
    bias = jnp.pad(jnp.concatenate([b_i, b_f]).astype(F32), (0, LANES - n_gate)).reshape(1, LANES)

    y, w1b, w2b = _front(x2, g_pre_mix.reshape(1, d), w_t, w_gate_t, conv_w.astype(F32), bias,
                         g_head.reshape(1, mw).astype(F32), w_mlp1, w_mlp2,
                         n_main=main, chunk=256, seq=seq, tn=256, cw=cw, hd=hd)
    x1 = _out_proj(y, w_out, x2, g_post_mix.reshape(1, d), tm=512)
    return _mlp(x1, g_pre_mlp.reshape(1, d), w1b, w2b, g_post_mlp.reshape(1, d),
                tm=1024, tf=1024, sub=256, mid_sub=512, vmem_limit=60 * MIB)


def kernel(x, g_pre_mix, g_post_mix, g_pre_mlp, g_post_mlp, w_in, conv_w, b_i, b_f, g_head,
           w_out, w_mlp1, w_mlp2):
    batch, seq, d = x.shape
    x2 = x.reshape(batch * seq, d)
    for layer in range(w_in.shape[0]):
        x2 = _layer(x2, g_pre_mix[layer], g_post_mix[layer], g_pre_mlp[layer], g_post_mlp[layer],
                    w_in[layer], conv_w[layer], b_i[layer], b_f[layer], g_head[layer],
                    w_out[layer], w_mlp1[layer], w_mlp2[layer], seq=seq)
    return x2.reshape(batch, seq, d)
```

```python
import functools

import jax
import jax.numpy as jnp
from jax import lax
from jax.experimental import pallas as pl
from jax.experimental.pallas import tpu as pltpu

EPS = 1e-6
CONV_K = 3
N_HEADS = 4
LANES = 128
SUBLANES = 8
BF16_ROWS = 16
NORM_ROWS = BF16_ROWS
CONV_ROWS = 2 * BF16_ROWS

F32 = jnp.float32
BF16 = jnp.bfloat16

MIB = 1024 * 1024
VMEM_LIMIT = 60 * MIB
TILES = dict(
    chunk=256,
    proj_cols=256,
    out_rows=1024,
    out_sub=256,
    mlp_rows=1024,
    mlp_hidden=1024,
    mlp_edge_sub=256,
    mlp_mid_sub=512,
)


def _rms_scale(t):
    return lax.rsqrt(jnp.mean(t * t, axis=-1, keepdims=True) + EPS)


def _dot_t(a, bt):
    return lax.dot_general(a, bt, (((1,), (1,)), ((), ())), preferred_element_type=F32)


def _merge_evenly(a, b):
    wa, wb = sum(c for c, _ in a), sum(c for c, _ in b)
    out, ia, ib, ca, cb = [], 0, 0, 0.0, 0.0
    while ia < len(a) or ib < len(b):
        if ib >= len(b) or (ia < len(a) and ca * wb <= cb * wa):
            out.append(a[ia])
            ca += a[ia][0]
            ia += 1
        else:
            out.append(b[ib])
            cb += b[ib][0]
            ib += 1
    return out


def _split3(a):
    hi = a.astype(BF16)
    r1 = a - hi.astype(F32)
    mid = r1.astype(BF16)
    lo = (r1 - mid.astype(F32)).astype(BF16)
    return hi, mid, lo


def _mix_stages(p_ref, gate_ref, convw_ref, bias_ref, ghead_ref, y_ref,
                cv_scr, ct_scr, m_scr, done, *, cw, hd):
    shared = {}

    def conv_stage(c):
        done(_mix_conv(c, p_ref, convw_ref, y_ref, cv_scr, cw=cw))

    def gate_stage():
        shared.update(_mix_gates(gate_ref, bias_ref, m_scr))

    def head_stages(h):
        local = {}

        def scores():
            local.update(_head_scores(h, p_ref, ct_scr, shared, cw=cw, hd=hd))
            done(local["scores"][0:SUBLANES, 0:LANES])

        def output():
            done(_head_output(h, p_ref, ghead_ref, y_ref, local, cw=cw, hd=hd))

        def state():
            done(_head_state(h, p_ref, ct_scr, shared, local, cw=cw, hd=hd))

        return [(2.0, scores), (3.0, output), (1.5, state)]

    convs = [(1.0, functools.partial(conv_stage, c)) for c in range(cw // LANES)]
    heads = [st for h in range(N_HEADS) for st in head_stages(h)]
    return [(2.0, gate_stage)] + _merge_evenly(heads, convs)


def _zero_after(tile):
    bits = lax.bitcast_convert_type(tile, jnp.uint32)
    bits = lax.shift_right_logical(lax.shift_right_logical(bits, jnp.uint32(16)), jnp.uint32(16))
    return lax.bitcast_convert_type(bits, F32)


def _anchor(res, tile):
    z = jnp.concatenate([_zero_after(tile)] * (res.shape[1] // LANES), axis=1)
    return jnp.concatenate([res[0:SUBLANES] + z, res[SUBLANES:]], axis=0)


def _mix_conv(c, p_ref, convw_ref, y_ref, cv_scr, *, cw):
    L = p_ref.shape[0]
    cols = slice(c * LANES, (c + 1) * LANES)
    w = convw_ref[:, cols]
    prev = cv_scr[:, cols]
    seen = jnp.zeros((SUBLANES, LANES), F32)
    for r in range(L // CONV_ROWS):
        rows = slice(r * CONV_ROWS, (r + 1) * CONV_ROWS)
        cx = p_ref[rows, c * LANES:(c + 1) * LANES].astype(F32)
        cb = p_ref[rows, cw + c * LANES:cw + (c + 1) * LANES].astype(F32)
        cc = p_ref[rows, 2 * cw + c * LANES:2 * cw + (c + 1) * LANES].astype(F32)
        u = cc * cx
        ext = jnp.concatenate([prev, u], axis=0)
        s1 = pltpu.roll(ext, 1, 0)[SUBLANES:]
        s2 = pltpu.roll(ext, 2, 0)[SUBLANES:]
        y_conv = cb * (w[0:1, :] * s2 + w[1:2, :] * s1 + w[2:3, :] * u)
        y_ref[rows, cols] = y_conv.astype(y_ref.dtype)
        prev = u[CONV_ROWS - SUBLANES:]
        seen = seen + y_conv[0:SUBLANES]
    cv_scr[:, cols] = prev
    return seen


def _mix_gates(gate_ref, bias_ref, m_scr):
    L = gate_ref.shape[0]
    a = gate_ref[...] + bias_ref[...]
    lane = lax.broadcasted_iota(jnp.int32, a.shape, 1)
    gl = jnp.where(lane < N_HEADS, a, jax.nn.log_sigmoid(a))
    i0 = lax.broadcasted_iota(jnp.int32, (L, L), 0)
    i1 = lax.broadcasted_iota(jnp.int32, (L, L), 1)
    tril = (i0 >= i1).astype(BF16)
    hi, mid, lo = _split3(gl)
    bc = (jnp.dot(tril, hi, preferred_element_type=F32)
          + jnp.dot(tril, mid, preferred_element_type=F32)
          + jnp.dot(tril, lo, preferred_element_type=F32))
    bsh = pltpu.roll(bc, LANES - N_HEADS, 1)
    ab = jnp.where(lane < N_HEADS, gl - bsh, 0.0)
    b_rows = bsh.T
    b_last = bsh[L - 1:L, :]
    m_prev_v = m_scr[0:1, :]
    m_new_v = b_last + jnp.maximum(m_prev_v, jnp.max(ab, axis=0, keepdims=True))
    decay_v = jnp.exp(b_last + m_prev_v - m_new_v)
    wgt = jnp.exp(ab + (b_last - m_new_v))
    m_scr[...] = jnp.broadcast_to(m_new_v, m_scr.shape)
    return dict(ab=ab, b_rows=b_rows, m_prev_v=m_prev_v, decay_v=decay_v, wgt=wgt,
                keep=i0 <= i1)


def _head_cols(h, *, cw, hd):
    mw = N_HEADS * hd
    return tuple(slice(3 * cw + i * mw + h * hd, 3 * cw + i * mw + (h + 1) * hd) for i in range(4))


def _head_scores(h, p_ref, ct_scr, shared, *, cw, hd):
    qs, ks, _, _ = _head_cols(h, cw=cw, hd=hd)
    q = p_ref[:, qs]
    k = p_ref[:, ks]
    b_row = shared["b_rows"][h:h + 1, :]
    a_col = shared["ab"][:, h:h + 1]
    m_prev = shared["m_prev_v"][:, h:h + 1]
    m_inter = b_row + m_prev
    log_d = jnp.where(shared["keep"], a_col + b_row, -jnp.inf)
    m_row = jnp.maximum(m_inter, jnp.max(log_d, axis=0, keepdims=True))
    scores = _dot_t(k, q) * jnp.exp(log_d - m_row)
    return dict(scores=scores, m_row=m_row, inter=jnp.exp(m_inter - m_row), ctn=ct_scr[h])


def _head_output(h, p_ref, ghead_ref, y_ref, local, *, cw, hd):
    qs, _, vs, os_ = _head_cols(h, cw=cw, hd=hd)
    q = p_ref[:, qs]
    o = p_ref[:, os_].astype(F32)
    scores, m_row, inter, ctn = local["scores"], local["m_row"], local["inter"], local["ctn"]
    qc = _dot_t(ctn.astype(BF16), q)
    v_t = p_ref[:, vs].astype(F32).T.astype(BF16)
    local["v_t"] = v_t
    num = jnp.dot(v_t, scores.astype(BF16), preferred_element_type=F32) + inter * qc[0:hd]
    den = jnp.sum(scores, axis=0, keepdims=True) + inter * qc[hd:hd + 1]
    floor = (hd ** 0.5) * jnp.exp(-m_row)
    h_t = num * (1.0 / jnp.maximum(jnp.abs(den), floor))
    hn_t = h_t * lax.rsqrt(jnp.mean(h_t * h_t, axis=0, keepdims=True) + EPS)
    y_h = jax.nn.sigmoid(o) * (hn_t.T * ghead_ref[:, h * hd:(h + 1) * hd])
    y_ref[:, cw + h * hd:cw + (h + 1) * hd] = y_h.astype(y_ref.dtype)
    return y_h[0:SUBLANES, 0:LANES]


def _head_state(h, p_ref, ct_scr, shared, local, *, cw, hd):
    _, ks, _, _ = _head_cols(h, cw=cw, hd=hd)
    ctn = local["ctn"]
    decay = shared["decay_v"][:, h:h + 1]
    kw = p_ref[:, ks].astype(F32) * shared["wgt"][:, h:h + 1]
    c_new = decay * ctn[0:hd] + jnp.dot(local["v_t"], kw.astype(BF16), preferred_element_type=F32)
    ct_scr[h, 0:hd, :] = c_new
    ct_scr[h, hd:hd + 1, :] = decay * ctn[hd:hd + 1] + jnp.sum(kw, axis=0, keepdims=True)
    return c_new[0:SUBLANES, 0:LANES]


def _front_kernel(x_ref, g_ref, wt_ref, wgt_ref, convw_ref, bias_ref, ghead_ref, *rest,
                  n_cast, nc, n_main, tn, cw, hd):
    cast_in, (y_ref, *cast_out) = rest[:n_cast], rest[n_cast:2 * n_cast + 1]
    u_a, u_b, p_a, p_b, g_a, g_b, cv_scr, ct_scr, m_scr = rest[2 * n_cast + 1:]
    s = pl.program_id(0)

    pending = []

    def done(tile):
        pending.append(tile)

    def anchored(res):
        if not pending:
            return res
        tile = functools.reduce(lambda a, b: a + b, pending)
        pending.clear()
        return _anchor(res, tile)

    def norm_stages(u_scr):
        def norm(r):
            rows = slice(r * NORM_ROWS, (r + 1) * NORM_ROWS)
            x = x_ref[rows, :]
            u = x * _rms_scale(x) * g_ref[...]
            u_scr[rows, :] = u.astype(BF16)
            done(u[0:SUBLANES, 0:LANES])

        return [(0.3, functools.partial(norm, r)) for r in range(x_ref.shape[0] // NORM_ROWS)]

    def project_stages(u_scr, p_scr, g_scr):
        def gate_cols():
            g_scr[...] = _dot_t(u_scr[...], wgt_ref[...])

        def main_cols(c):
            cols = slice(c * tn, (c + 1) * tn)
            p_scr[:, cols] = anchored(_dot_t(u_scr[...], wt_ref[cols, :])).astype(BF16)

        return ([(0.5, gate_cols)]
                + [(1.0, functools.partial(main_cols, c)) for c in range(n_main // tn)])

    def mix_stages(p_scr, g_scr):
        def cast_stage():
            for src, dst in zip(cast_in, cast_out):
                dst[...] = src[...].astype(BF16)

        return [(0.5, cast_stage)] + _mix_stages(p_scr, g_scr, convw_ref, bias_ref, ghead_ref, y_ref,
                                          cv_scr, ct_scr, m_scr, done, cw=cw, hd=hd)

    def run_interleaved(*streams):
        *vector_streams, matmul_stream = streams
        merged = []
        for st in vector_streams:
            merged = _merge_evenly(merged, st)
        n_tail = len(matmul_stream) // 8 if merged else 0
        split = len(matmul_stream) - n_tail
        merged = _merge_evenly(matmul_stream[:split], merged) + matmul_stream[split:]
        pending.clear()
        for _, stage in merged:
            stage()
        pending.clear()

    def zero(*refs):
        for ref in refs:
            ref[...] = jnp.zeros_like(ref)

    @pl.when(s == 0)
    def _():
        zero(u_a, u_b, p_a, p_b, g_a, g_b, cv_scr, ct_scr, m_scr)

    @pl.when(jnp.logical_and(s >= 2, lax.rem(s - 2, nc) == 0))
    def _():
        zero(cv_scr, ct_scr, m_scr)

    even = lax.rem(s, 2) == 0

    @pl.when(even)
    def _():
        run_interleaved(mix_stages(p_a, g_a), norm_stages(u_a), project_stages(u_b, p_b, g_b))

    @pl.when(jnp.logical_not(even))
    def _():
        run_interleaved(mix_stages(p_b, g_b), norm_stages(u_b), project_stages(u_a, p_a, g_a))


def _front(x2, g, w_t, w_gate_t, conv_w, bias, g_head, to_cast, *, n_main, chunk, seq, tn, cw, hd):
    m, d = x2.shape
    mw = N_HEADS * hd
    n_chunks = m // chunk
    nc = seq // chunk
    assert n_chunks % 2 == 0 and n_main % tn == 0 and n_main <= w_t.shape[0]
    slabs = [(w.shape[0] // n_chunks, w.shape[1]) for w in to_cast]
    assert all(r * n_chunks == w.shape[0] and r % BF16_ROWS == 0
               for (r, _), w in zip(slabs, to_cast))
    norm_blk = lambda s: (jnp.minimum(s, n_chunks - 1), 0)
    mix_blk = lambda s: (jnp.clip(s - 2, 0, n_chunks - 1), 0)
    whole = lambda s: (0, 0)
    kern = functools.partial(_front_kernel, n_cast=len(to_cast), nc=nc, n_main=n_main, tn=tn,
                             cw=cw, hd=hd)
    return pl.pallas_call(
        kern,
        name="front",
        grid=(n_chunks + 2,),
        in_specs=[
            pl.BlockSpec((chunk, d), norm_blk),
            pl.BlockSpec((1, d), whole),
            pl.BlockSpec(w_t.shape, whole, pipeline_mode=pl.Buffered(1)),
            pl.BlockSpec((LANES, d), whole, pipeline_mode=pl.Buffered(1)),
            pl.BlockSpec((CONV_K, cw), whole),
            pl.BlockSpec((1, LANES), whole),
            pl.BlockSpec((1, mw), whole),
        ] + [pl.BlockSpec(slab, mix_blk) for slab in slabs],
        out_specs=[pl.BlockSpec((chunk, cw + mw), mix_blk)]
        + [pl.BlockSpec(slab, mix_blk) for slab in slabs],
        out_shape=[jax.ShapeDtypeStruct((m, cw + mw), BF16)]
        + [jax.ShapeDtypeStruct(w.shape, BF16) for w in to_cast],
        scratch_shapes=[
            pltpu.VMEM((chunk, d), BF16),
            pltpu.VMEM((chunk, d), BF16),
            pltpu.VMEM((chunk, n_main), BF16),
            pltpu.VMEM((chunk, n_main), BF16),
            pltpu.VMEM((chunk, LANES), F32),
            pltpu.VMEM((chunk, LANES), F32),
            pltpu.VMEM((SUBLANES, cw), F32),
            pltpu.VMEM((N_HEADS, hd + BF16_ROWS, hd), F32),
            pltpu.VMEM((SUBLANES, LANES), F32),
        ],
        compiler_params=pltpu.CompilerParams(
            dimension_semantics=("arbitrary",),
            vmem_limit_bytes=VMEM_LIMIT),
    )(x2, g, w_t, w_gate_t, conv_w, bias, g_head, *to_cast)


def _out_proj_kernel(y_ref, w_ref, x_ref, g_ref, o_ref, *, sub):
    prev = None
    for r in range(y_ref.shape[0] // sub):
        rows = slice(r * sub, (r + 1) * sub)
        t = jnp.dot(y_ref[rows, :], w_ref[...], preferred_element_type=F32)
        if prev is not None:
            t = _anchor(t, prev)
        out = x_ref[rows, :] + t * _rms_scale(t) * g_ref[...]
        o_ref[rows, :] = out
        prev = out[0:SUBLANES, 0:LANES]


def _out_proj(y, w_out, x2, g, *, tm, sub, vmem_limit):
    m, d = x2.shape
    e = y.shape[1]
    return pl.pallas_call(
        functools.partial(_out_proj_kernel, sub=sub),
        name="out_proj",
        grid=(m // tm,),
        in_specs=[
            pl.BlockSpec((tm, e), lambda i: (i, 0)),
            pl.BlockSpec((e, d), lambda i: (0, 0), pipeline_mode=pl.Buffered(1)),
            pl.BlockSpec((tm, d), lambda i: (i, 0)),
            pl.BlockSpec((1, d), lambda i: (0, 0)),
        ],
        out_specs=pl.BlockSpec((tm, d), lambda i: (i, 0)),
        out_shape=jax.ShapeDtypeStruct((m, d), F32),
        compiler_params=pltpu.CompilerParams(
            dimension_semantics=("arbitrary",),
            vmem_limit_bytes=vmem_limit),
    )(y, w_out, x2, g)


def _mlp_kernel(x_ref, g1_ref, w1_ref, w2_ref, g2_ref, o_ref, z_scr, *, sub, mid_sub):
    f = pl.program_id(1)
    last = pl.num_programs(1) - 1
    n_sub = x_ref.shape[0] // sub

    def ffn(z):
        h = jnp.dot(z, w1_ref[...], preferred_element_type=F32)
        h = jnp.square(jnp.maximum(h, 0.0)).astype(BF16)
        return jnp.dot(h, w2_ref[...], preferred_element_type=F32)

    @pl.when(f == 0)
    def _():
        for r in range(n_sub):
            rows = slice(r * sub, (r + 1) * sub)
            x = x_ref[rows, :]
            z = (x * _rms_scale(x) * g1_ref[...]).astype(BF16)
            z_scr[rows, :] = z
            o_ref[rows, :] = ffn(z)

    @pl.when(jnp.logical_and(f > 0, f < last))
    def _():
        for r in range(x_ref.shape[0] // mid_sub):
            rows = slice(r * mid_sub, (r + 1) * mid_sub)
            o_ref[rows, :] += ffn(z_scr[rows, :])

    @pl.when(f == last)
    def _():
        for r in range(n_sub):
            rows = slice(r * sub, (r + 1) * sub)
            t = o_ref[rows, :] + ffn(z_scr[rows, :])
            o_ref[rows, :] = x_ref[rows, :] + t * _rms_scale(t) * g2_ref[...]


def _mlp(x1, g1, w1, w2, g2, *, tm, tf, sub, mid_sub, vmem_limit):
    m, d = x1.shape
    dff = w1.shape[1]
    assert dff // tf >= 2
    return pl.pallas_call(
        functools.partial(_mlp_kernel, sub=sub, mid_sub=mid_sub),
        name="mlp",
        grid=(m // tm, dff // tf),
        in_specs=[
            pl.BlockSpec((tm, d), lambda i, f: (i, 0)),
            pl.BlockSpec((1, d), lambda i, f: (0, 0)),
            pl.BlockSpec((d, tf), lambda i, f: (0, f)),
            pl.BlockSpec((tf, d), lambda i, f: (f, 0)),
            pl.BlockSpec((1, d), lambda i, f: (0, 0)),
        ],
        out_specs=pl.BlockSpec((tm, d), lambda i, f: (i, 0)),
        out_shape=jax.ShapeDtypeStruct((m, d), F32),
        scratch_shapes=[pltpu.VMEM((tm, d), BF16)],
        compiler_params=pltpu.CompilerParams(
            dimension_semantics=("arbitrary", "arbitrary"),
            vmem_limit_bytes=vmem_limit),
    )(x1, g1, w1, w2, g2)


def _layer(x2, g_pre_mix, g_post_mix, g_pre_mlp, g_post_mlp, w_in, conv_w, b_i, b_f,
           g_head, w_out, w_mlp1, w_mlp2, *, seq):
    d = x2.shape[1]
    cw = conv_w.shape[1]
    mw = g_head.shape[0]
    hd = mw // N_HEADS
    main = 3 * cw + 4 * mw
    n_gate = 2 * N_HEADS

    w_t = jnp.swapaxes(w_in, 0, 1).astype(BF16)
    w_gate_t = jnp.pad(w_t[main:main + n_gate], ((0, LANES - n_gate), (0, 0)))
    bias = jnp.pad(jnp.concatenate([b_i, b_f]).astype(F32), (0, LANES - n_gate)).reshape(1, LANES)

    t = TILES
    y, w1b, w2b, w_out_b = _front(
        x2, g_pre_mix.reshape(1, d), w_t, w_gate_t, conv_w.astype(F32), bias,
        g_head.reshape(1, mw).astype(F32), [w_mlp1, w_mlp2, w_out],
        n_main=main, chunk=t["chunk"], seq=seq, tn=t["proj_cols"], cw=cw, hd=hd)
    x1 = _out_proj(y, w_out_b, x2, g_post_mix.reshape(1, d), tm=t["out_rows"], sub=t["out_sub"],
                   vmem_limit=VMEM_LIMIT)
    return _mlp(x1, g_pre_mlp.reshape(1, d), w1b, w2b, g_post_mlp.reshape(1, d),
                tm=t["mlp_rows"], tf=t["mlp_hidden"], sub=t["mlp_edge_sub"],
                mid_sub=t["mlp_mid_sub"], vmem_limit=VMEM_LIMIT)


def kernel(x, g_pre_mix, g_post_mix, g_pre_mlp, g_post_mlp, w_in, conv_w, b_i, b_f, g_head,
           w_out, w_mlp1, w_mlp2):
    batch, seq, d = x.shape
    x2 = x.reshape(batch * seq, d)
    for layer in range(w_in.shape[0]):
        x2 = _layer(x2, g_pre_mix[layer], g_post_mix[layer], g_pre_mlp[layer], g_post_mlp[layer],
                    w_in[layer], conv_w[layer], b_i[layer], b_f[layer], g_head[layer],
                    w_out[layer], w_mlp1[layer], w_mlp2[layer], seq=seq)
    return x2.reshape(batch, seq, d)
```

```python
import functools

import jax
import jax.numpy as jnp
from jax import lax
from jax.experimental import pallas as pl
from jax.experimental.pallas import tpu as pltpu

EPS = 1e-6
CONV_K = 3
N_HEADS = 4
LANES = 128
SUBLANES = 8
BF16_ROWS = 16
NORM_ROWS = BF16_ROWS
CONV_ROWS = 2 * BF16_ROWS

F32 = jnp.float32
BF16 = jnp.bfloat16

MIB = 1024 * 1024
VMEM_LIMIT = 60 * MIB
TILES = dict(
    chunk=256,
    proj_cols=256,
    out_rows=1024,
    mlp_rows=1024,
    mlp_hidden=1024,
    mlp_edge_sub=256,
    mlp_mid_sub=512,
)


def _rms_scale(t):
    return lax.rsqrt(jnp.mean(t * t, axis=-1, keepdims=True) + EPS)


def _dot_t(a, bt):
    return lax.dot_general(a, bt, (((1,), (1,)), ((), ())), preferred_element_type=F32)


def _merge_evenly(a, b):
    wa, wb = sum(c for c, _ in a), sum(c for c, _ in b)
    out, ia, ib, ca, cb = [], 0, 0, 0.0, 0.0
    while ia < len(a) or ib < len(b):
        if ib >= len(b) or (ia < len(a) and ca * wb <= cb * wa):
            out.append(a[ia])
            ca += a[ia][0]
            ia += 1
        else:
            out.append(b[ib])
            cb += b[ib][0]
            ib += 1
    return out


def _split3(a):
    hi = a.astype(BF16)
    r1 = a - hi.astype(F32)
    mid = r1.astype(BF16)
    lo = (r1 - mid.astype(F32)).astype(BF16)
    return hi, mid, lo


def _mix_stages(p_ref, gate_ref, convw_ref, bias_ref, ghead_ref, y_ref,
                cv_scr, ct_scr, m_scr, done, *, cw, hd):
    shared = {}

    def conv_stage(c):
        done(_mix_conv(c, p_ref, convw_ref, y_ref, cv_scr, cw=cw))

    def gate_stage():
        shared.update(_mix_gates(gate_ref, bias_ref, m_scr))

    def head_stages(h):
        local = {}

        def scores():
            local.update(_head_scores(h, p_ref, ct_scr, shared, cw=cw, hd=hd))
            done(local["scores"][0:SUBLANES, 0:LANES])

        def output():
            done(_head_output(h, p_ref, ghead_ref, y_ref, local, cw=cw, hd=hd))

        def state():
            done(_head_state(h, p_ref, ct_scr, shared, local, cw=cw, hd=hd))

        return [(2.0, scores), (3.0, output), (1.5, state)]

    convs = [(1.0, functools.partial(conv_stage, c)) for c in range(cw // LANES)]
    heads = [st for h in range(N_HEADS) for st in head_stages(h)]
    return [(2.0, gate_stage)] + _merge_evenly(heads, convs)


def _zero_after(tile):
    bits = lax.bitcast_convert_type(tile, jnp.uint32)
    bits = lax.shift_right_logical(lax.shift_right_logical(bits, jnp.uint32(16)), jnp.uint32(16))
    return lax.bitcast_convert_type(bits, F32)


def _anchor(res, tile):
    z = jnp.concatenate([_zero_after(tile)] * (res.shape[1] // LANES), axis=1)
    return jnp.concatenate([res[0:SUBLANES] + z, res[SUBLANES:]], axis=0)


def _mix_conv(c, p_ref, convw_ref, y_ref, cv_scr, *, cw):
    L = p_ref.shape[0]
    cols = slice(c * LANES, (c + 1) * LANES)
    w = convw_ref[:, cols]
    prev = cv_scr[:, cols]
    seen = jnp.zeros((SUBLANES, LANES), F32)
    for r in range(L // CONV_ROWS):
        rows = slice(r * CONV_ROWS, (r + 1) * CONV_ROWS)
        cx = p_ref[rows, c * LANES:(c + 1) * LANES].astype(F32)
        cb = p_ref[rows, cw + c * LANES:cw + (c + 1) * LANES].astype(F32)
        cc = p_ref[rows, 2 * cw + c * LANES:2 * cw + (c + 1) * LANES].astype(F32)
        u = cc * cx
        ext = jnp.concatenate([prev, u], axis=0)
        s1 = pltpu.roll(ext, 1, 0)[SUBLANES:]
        s2 = pltpu.roll(ext, 2, 0)[SUBLANES:]
        y_conv = cb * (w[0:1, :] * s2 + w[1:2, :] * s1 + w[2:3, :] * u)
        y_ref[rows, cols] = y_conv.astype(y_ref.dtype)
        prev = u[CONV_ROWS - SUBLANES:]
        seen = seen + y_conv[0:SUBLANES]
    cv_scr[:, cols] = prev
    return seen


def _mix_gates(gate_ref, bias_ref, m_scr):
    L = gate_ref.shape[0]
    a = gate_ref[...] + bias_ref[...]
    lane = lax.broadcasted_iota(jnp.int32, a.shape, 1)
    gl = jnp.where(lane < N_HEADS, a, jax.nn.log_sigmoid(a))
    i0 = lax.broadcasted_iota(jnp.int32, (L, L), 0)
    i1 = lax.broadcasted_iota(jnp.int32, (L, L), 1)
    tril = (i0 >= i1).astype(BF16)
    hi, mid, lo = _split3(gl)
    bc = (jnp.dot(tril, hi, preferred_element_type=F32)
          + jnp.dot(tril, mid, preferred_element_type=F32)
          + jnp.dot(tril, lo, preferred_element_type=F32))
    bsh = pltpu.roll(bc, LANES - N_HEADS, 1)
    ab = jnp.where(lane < N_HEADS, gl - bsh, 0.0)
    b_rows = bsh.T
    b_last = bsh[L - 1:L, :]
    m_prev_v = m_scr[0:1, :]
    m_new_v = b_last + jnp.maximum(m_prev_v, jnp.max(ab, axis=0, keepdims=True))
    decay_v = jnp.exp(b_last + m_prev_v - m_new_v)
    wgt = jnp.exp(ab + (b_last - m_new_v))
    m_scr[...] = jnp.broadcast_to(m_new_v, m_scr.shape)
    return dict(ab=ab, b_rows=b_rows, m_prev_v=m_prev_v, decay_v=decay_v, wgt=wgt,
                keep=i0 <= i1)


def _head_cols(h, *, cw, hd):
    mw = N_HEADS * hd
    return tuple(slice(3 * cw + i * mw + h * hd, 3 * cw + i * mw + (h + 1) * hd) for i in range(4))


def _head_scores(h, p_ref, ct_scr, shared, *, cw, hd):
    qs, ks, _, _ = _head_cols(h, cw=cw, hd=hd)
    q = p_ref[:, qs]
    k = p_ref[:, ks]
    b_row = shared["b_rows"][h:h + 1, :]
    a_col = shared["ab"][:, h:h + 1]
    m_prev = shared["m_prev_v"][:, h:h + 1]
    m_inter = b_row + m_prev
    log_d = jnp.where(shared["keep"], a_col + b_row, -jnp.inf)
    m_row = jnp.maximum(m_inter, jnp.max(log_d, axis=0, keepdims=True))
    scores = _dot_t(k, q) * jnp.exp(log_d - m_row)
    return dict(scores=scores, m_row=m_row, inter=jnp.exp(m_inter - m_row), ctn=ct_scr[h])


def _head_output(h, p_ref, ghead_ref, y_ref, local, *, cw, hd):
    qs, _, vs, os_ = _head_cols(h, cw=cw, hd=hd)
    q = p_ref[:, qs]
    o = p_ref[:, os_].astype(F32)
    scores, m_row, inter, ctn = local["scores"], local["m_row"], local["inter"], local["ctn"]
    qc = _dot_t(ctn.astype(BF16), q)
    v_t = p_ref[:, vs].astype(F32).T.astype(BF16)
    local["v_t"] = v_t
    num = jnp.dot(v_t, scores.astype(BF16), preferred_element_type=F32) + inter * qc[0:hd]
    den = jnp.sum(scores, axis=0, keepdims=True) + inter * qc[hd:hd + 1]
    floor = (hd ** 0.5) * jnp.exp(-m_row)
    h_t = num * (1.0 / jnp.maximum(jnp.abs(den), floor))
    hn_t = h_t * lax.rsqrt(jnp.mean(h_t * h_t, axis=0, keepdims=True) + EPS)
    y_h = jax.nn.sigmoid(o) * (hn_t.T * ghead_ref[:, h * hd:(h + 1) * hd])
    y_ref[:, cw + h * hd:cw + (h + 1) * hd] = y_h.astype(y_ref.dtype)
    return y_h[0:SUBLANES, 0:LANES]


def _head_state(h, p_ref, ct_scr, shared, local, *, cw, hd):
    _, ks, _, _ = _head_cols(h, cw=cw, hd=hd)
    ctn = local["ctn"]
    decay = shared["decay_v"][:, h:h + 1]
    kw = p_ref[:, ks].astype(F32) * shared["wgt"][:, h:h + 1]
    c_new = decay * ctn[0:hd] + jnp.dot(local["v_t"], kw.astype(BF16), preferred_element_type=F32)
    ct_scr[h, 0:hd, :] = c_new
    ct_scr[h, hd:hd + 1, :] = decay * ctn[hd:hd + 1] + jnp.sum(kw, axis=0, keepdims=True)
    return c_new[0:SUBLANES, 0:LANES]


def _front_kernel(x_ref, g_ref, wt_ref, wgt_ref, convw_ref, bias_ref, ghead_ref, *rest,
                  n_cast, nc, n_main, tn, cw, hd):
    cast_in, (y_ref, *cast_out) = rest[:n_cast], rest[n_cast:2 * n_cast + 1]
    u_a, u_b, p_a, p_b, g_a, g_b, cv_scr, ct_scr, m_scr = rest[2 * n_cast + 1:]
    s = pl.program_id(0)

    pending = []

    def done(tile):
        pending.append(tile)

    def anchored(res):
        if not pending:
            return res
        tile = functools.reduce(lambda a, b: a + b, pending)
        pending.clear()
        return _anchor(res, tile)

    def norm_stages(u_scr):
        def norm(r):
            rows = slice(r * NORM_ROWS, (r + 1) * NORM_ROWS)
            x = x_ref[rows, :]
            u = x * _rms_scale(x) * g_ref[...]
            u_scr[rows, :] = u.astype(BF16)
            done(u[0:SUBLANES, 0:LANES])

        return [(0.3, functools.partial(norm, r)) for r in range(x_ref.shape[0] // NORM_ROWS)]

    def project_stages(u_scr, p_scr, g_scr):
        def gate_cols():
            g_scr[...] = _dot_t(u_scr[...], wgt_ref[...])

        def main_cols(c):
            cols = slice(c * tn, (c + 1) * tn)
            p_scr[:, cols] = anchored(_dot_t(u_scr[...], wt_ref[cols, :])).astype(BF16)

        return ([(0.5, gate_cols)]
                + [(1.0, functools.partial(main_cols, c)) for c in range(n_main // tn)])

    def mix_stages(p_scr, g_scr):
        def cast_stage():
            for src, dst in zip(cast_in, cast_out):
                dst[...] = src[...].astype(BF16)

        return [(0.5, cast_stage)] + _mix_stages(p_scr, g_scr, convw_ref, bias_ref, ghead_ref, y_ref,
                                          cv_scr, ct_scr, m_scr, done, cw=cw, hd=hd)

    def run_interleaved(*streams):
        *vector_streams, matmul_stream = streams
        merged = []
        for st in vector_streams:
            merged = _merge_evenly(merged, st)
        n_tail = len(matmul_stream) // 8 if merged else 0
        split = len(matmul_stream) - n_tail
        merged = _merge_evenly(matmul_stream[:split], merged) + matmul_stream[split:]
        pending.clear()
        for _, stage in merged:
            stage()
        pending.clear()

    def zero(*refs):
        for ref in refs:
            ref[...] = jnp.zeros_like(ref)

    @pl.when(s == 0)
    def _():
        zero(u_a, u_b, p_a, p_b, g_a, g_b, cv_scr, ct_scr, m_scr)

    @pl.when(jnp.logical_and(s >= 2, lax.rem(s - 2, nc) == 0))
    def _():
        zero(cv_scr, ct_scr, m_scr)

    even = lax.rem(s, 2) == 0

    @pl.when(even)
    def _():
        run_interleaved(mix_stages(p_a, g_a), norm_stages(u_a), project_stages(u_b, p_b, g_b))

    @pl.when(jnp.logical_not(even))
    def _():
        run_interleaved(mix_stages(p_b, g_b), norm_stages(u_b), project_stages(u_a, p_a, g_a))


def _front(x2, g, w_t, w_gate_t, conv_w, bias, g_head, to_cast, *, n_main, chunk, seq, tn, cw, hd):
    m, d = x2.shape
    mw = N_HEADS * hd
    n_chunks = m // chunk
    nc = seq // chunk
    assert n_chunks % 2 == 0 and n_main % tn == 0 and n_main <= w_t.shape[0]
    slabs = [(w.shape[0] // n_chunks, w.shape[1]) for w in to_cast]
    assert all(r * n_chunks == w.shape[0] and r % BF16_ROWS == 0
               for (r, _), w in zip(slabs, to_cast))
    norm_blk = lambda s: (jnp.minimum(s, n_chunks - 1), 0)
    mix_blk = lambda s: (jnp.clip(s - 2, 0, n_chunks - 1), 0)
    whole = lambda s: (0, 0)
    kern = functools.partial(_front_kernel, n_cast=len(to_cast), nc=nc, n_main=n_main, tn=tn,
                             cw=cw, hd=hd)
    return pl.pallas_call(
        kern,
        name="front",
        grid=(n_chunks + 2,),
        in_specs=[
            pl.BlockSpec((chunk, d), norm_blk),
            pl.BlockSpec((1, d), whole),
            pl.BlockSpec(w_t.shape, whole, pipeline_mode=pl.Buffered(1)),
            pl.BlockSpec((LANES, d), whole, pipeline_mode=pl.Buffered(1)),
            pl.BlockSpec((CONV_K, cw), whole),
            pl.BlockSpec((1, LANES), whole),
            pl.BlockSpec((1, mw), whole),
        ] + [pl.BlockSpec(slab, mix_blk) for slab in slabs],
        out_specs=[pl.BlockSpec((chunk, cw + mw), mix_blk)]
        + [pl.BlockSpec(slab, mix_blk) for slab in slabs],
        out_shape=[jax.ShapeDtypeStruct((m, cw + mw), BF16)]
        + [jax.ShapeDtypeStruct(w.shape, BF16) for w in to_cast],
        scratch_shapes=[
            pltpu.VMEM((chunk, d), BF16),
            pltpu.VMEM((chunk, d), BF16),
            pltpu.VMEM((chunk, n_main), BF16),
            pltpu.VMEM((chunk, n_main), BF16),
            pltpu.VMEM((chunk, LANES), F32),
            pltpu.VMEM((chunk, LANES), F32),
            pltpu.VMEM((SUBLANES, cw), F32),
            pltpu.VMEM((N_HEADS, hd + BF16_ROWS, hd), F32),
            pltpu.VMEM((SUBLANES, LANES), F32),
        ],
        compiler_params=pltpu.CompilerParams(
            dimension_semantics=("arbitrary",),
            vmem_limit_bytes=VMEM_LIMIT),
    )(x2, g, w_t, w_gate_t, conv_w, bias, g_head, *to_cast)


def _out_proj_kernel(y_ref, w_ref, x_ref, g_ref, o_ref):
    t = jnp.dot(y_ref[...], w_ref[...], preferred_element_type=F32)
    o_ref[...] = x_ref[...] + t * _rms_scale(t) * g_ref[...]


def _out_proj(y, w_out, x2, g, *, tm, vmem_limit):
    m, d = x2.shape
    e = y.shape[1]
    return pl.pallas_call(
        _out_proj_kernel,
        name="out_proj",
        grid=(m // tm,),
        in_specs=[
            pl.BlockSpec((tm, e), lambda i: (i, 0)),
            pl.BlockSpec((e, d), lambda i: (0, 0), pipeline_mode=pl.Buffered(1)),
            pl.BlockSpec((tm, d), lambda i: (i, 0)),
            pl.BlockSpec((1, d), lambda i: (0, 0)),
        ],
        out_specs=pl.BlockSpec((tm, d), lambda i: (i, 0)),
        out_shape=jax.ShapeDtypeStruct((m, d), F32),
        compiler_params=pltpu.CompilerParams(
            dimension_semantics=("arbitrary",),
            vmem_limit_bytes=vmem_limit),
    )(y, w_out, x2, g)


def _mlp_kernel(x_ref, g1_ref, w1_hbm, w2_hbm, g2_ref, o_ref, z_scr, w1_buf, w2_buf, sem,
                *, tf, nf, sub, mid_sub):
    i = pl.program_id(0)
    last_i = pl.num_programs(0) - 1
    n_sub = x_ref.shape[0] // sub

    def w_copies(f, slot):
        return (pltpu.make_async_copy(w1_hbm.at[:, pl.ds(f * tf, tf)], w1_buf.at[slot],
                                      sem.at[0, slot]),
                pltpu.make_async_copy(w2_hbm.at[pl.ds(f * tf, tf), :], w2_buf.at[slot],
                                      sem.at[1, slot]))

    def start(f, slot):
        for c in w_copies(f, slot):
            c.start()

    def wait(f, slot):
        for c in w_copies(f, slot):
            c.wait()

    def ffn(z, slot):
        h = jnp.dot(z, w1_buf[slot], preferred_element_type=F32)
        h = jnp.square(jnp.maximum(h, 0.0)).astype(BF16)
        return jnp.dot(h, w2_buf[slot], preferred_element_type=F32)

    @pl.when(i == 0)
    def _():
        start(0, 0)

    start(1, 1)
    wait(0, 0)
    for r in range(n_sub):
        rows = slice(r * sub, (r + 1) * sub)
        x = x_ref[rows, :]
        z = (x * _rms_scale(x) * g1_ref[...]).astype(BF16)
        z_scr[rows, :] = z
        o_ref[rows, :] = ffn(z, 0)

    def middle(f, carry):
        slot = lax.rem(f, 2)
        start(f + 1, 1 - slot)
        wait(f, slot)
        for r in range(x_ref.shape[0] // mid_sub):
            rows = slice(r * mid_sub, (r + 1) * mid_sub)
            o_ref[rows, :] += ffn(z_scr[rows, :], slot)
        return carry

    lax.fori_loop(1, nf - 1, middle, 0)

    @pl.when(i < last_i)
    def _():
        start(0, 0)

    last_slot = (nf - 1) % 2
    wait(nf - 1, last_slot)
    for r in range(n_sub):
        rows = slice(r * sub, (r + 1) * sub)
        t = o_ref[rows, :] + ffn(z_scr[rows, :], last_slot)
        o_ref[rows, :] = x_ref[rows, :] + t * _rms_scale(t) * g2_ref[...]


def _mlp(x1, g1, w1, w2, g2, *, tm, tf, sub, mid_sub, vmem_limit):
    m, d = x1.shape
    dff = w1.shape[1]
    nf = dff // tf
    assert nf >= 3 and nf % 2 == 0
    return pl.pallas_call(
        functools.partial(_mlp_kernel, tf=tf, nf=nf, sub=sub, mid_sub=mid_sub),
        name="mlp",
        grid=(m // tm,),
        in_specs=[
            pl.BlockSpec((tm, d), lambda i: (i, 0)),
            pl.BlockSpec((1, d), lambda i: (0, 0)),
            pl.BlockSpec(memory_space=pl.ANY),
            pl.BlockSpec(memory_space=pl.ANY),
            pl.BlockSpec((1, d), lambda i: (0, 0)),
        ],
        out_specs=pl.BlockSpec((tm, d), lambda i: (i, 0)),
        out_shape=jax.ShapeDtypeStruct((m, d), F32),
        scratch_shapes=[
            pltpu.VMEM((tm, d), BF16),
            pltpu.VMEM((2, d, tf), BF16),
            pltpu.VMEM((2, tf, d), BF16),
            pltpu.SemaphoreType.DMA((2, 2)),
        ],
        compiler_params=pltpu.CompilerParams(
            dimension_semantics=("arbitrary",),
            vmem_limit_bytes=vmem_limit),
    )(x1, g1, w1, w2, g2)


def _layer(x2, g_pre_mix, g_post_mix, g_pre_mlp, g_post_mlp, w_in, conv_w, b_i, b_f,
           g_head, w_out, w_mlp1, w_mlp2, *, seq):
    d = x2.shape[1]
    cw = conv_w.shape[1]
    mw = g_head.shape[0]
    hd = mw // N_HEADS
    main = 3 * cw + 4 * mw
    n_gate = 2 * N_HEADS

    w_t = jnp.swapaxes(w_in, 0, 1).astype(BF16)
    w_gate_t = jnp.pad(w_t[main:main + n_gate], ((0, LANES - n_gate), (0, 0)))
    bias = jnp.pad(jnp.concatenate([b_i, b_f]).astype(F32), (0, LANES - n_gate)).reshape(1, LANES)

    t = TILES
    y, w1b, w2b, w_out_b = _front(
        x2, g_pre_mix.reshape(1, d), w_t, w_gate_t, conv_w.astype(F32), bias,
        g_head.reshape(1, mw).astype(F32), [w_mlp1, w_mlp2, w_out],
        n_main=main, chunk=t["chunk"], seq=seq, tn=t["proj_cols"], cw=cw, hd=hd)
    x1 = _out_proj(y, w_out_b, x2, g_post_mix.reshape(1, d), tm=t["out_rows"],
                   vmem_limit=VMEM_LIMIT)
    return _mlp(x1, g_pre_mlp.reshape(1, d), w1b, w2b, g_post_mlp.reshape(1, d),
                tm=t["mlp_rows"], tf=t["mlp_hidden"], sub=t["mlp_edge_sub"],
                mid_sub=t["mlp_mid_sub"], vmem_limit=VMEM_LIMIT)


def kernel(x, g_pre_mix, g_post_mix, g_pre_mlp, g_post_mlp, w_in, conv_w, b_i, b_f, g_head,
           w_out, w_mlp1, w_mlp2):
    batch, seq, d = x.shape
    x2 = x.reshape(batch * seq, d)
    for layer in range(w_in.shape[0]):
        x2 = _layer(x2, g_pre_mix[layer], g_post_mix[layer], g_pre_mlp[layer], g_post_mlp[layer],
                    w_in[layer], conv_w[layer], b_i[layer], b_f[layer], g_head[layer],
                    w_out[layer], w_mlp1[layer], w_mlp2[layer], seq=seq)
    return x2.reshape(batch, seq, d)
```

```python
import functools

import jax
import jax.numpy as jnp
from jax import lax
from jax.experimental import pallas as pl
from jax.experimental.pallas import tpu as pltpu

EPS = 1e-6
CONV_K = 3
N_HEADS = 4
LANES = 128
SUBLANES = 8
BF16_ROWS = 16
NORM_ROWS = BF16_ROWS
CONV_ROWS = 2 * BF16_ROWS

F32 = jnp.float32
BF16 = jnp.bfloat16

MIB = 1024 * 1024
VMEM_LIMIT = 60 * MIB
TILES = dict(
    chunk=256,
    proj_cols=256,
    out_rows=1024,
    mlp_rows=1024,
    mlp_hidden=1024,
    mlp_edge_sub=256,
    mlp_mid_sub=512,
)


def _rms_scale(t):
    return lax.rsqrt(jnp.mean(t * t, axis=-1, keepdims=True) + EPS)


def _dot_t(a, bt):
    return lax.dot_general(a, bt, (((1,), (1,)), ((), ())), preferred_element_type=F32)


def _merge_evenly(a, b):
    wa, wb = sum(c for c, _ in a), sum(c for c, _ in b)
    out, ia, ib, ca, cb = [], 0, 0, 0.0, 0.0
    while ia < len(a) or ib < len(b):
        if ib >= len(b) or (ia < len(a) and ca * wb <= cb * wa):
            out.append(a[ia])
            ca += a[ia][0]
            ia += 1
        else:
            out.append(b[ib])
            cb += b[ib][0]
            ib += 1
    return out


def _split3(a):
    hi = a.astype(BF16)
    r1 = a - hi.astype(F32)
    mid = r1.astype(BF16)
    lo = (r1 - mid.astype(F32)).astype(BF16)
    return hi, mid, lo


def _mix_stages(p_ref, gate_ref, convw_ref, bias_ref, ghead_ref, y_ref,
                cv_scr, ct_scr, m_scr, done, *, cw, hd):
    shared = {}

    def conv_stage(c):
        done(_mix_conv(c, p_ref, convw_ref, y_ref, cv_scr, cw=cw))

    def gate_stage():
        shared.update(_mix_gates(gate_ref, bias_ref, m_scr))

    def head_stages(h):
        local = {}

        def scores():
            local.update(_head_scores(h, p_ref, ct_scr, shared, cw=cw, hd=hd))
            done(local["scores"][0:SUBLANES, 0:LANES])

        def output():
            done(_head_output(h, p_ref, ghead_ref, y_ref, local, cw=cw, hd=hd))

        def state():
            done(_head_state(h, p_ref, ct_scr, shared, local, cw=cw, hd=hd))

        return [(2.0, scores), (3.0, output), (1.5, state)]

    convs = [(1.0, functools.partial(conv_stage, c)) for c in range(cw // LANES)]
    heads = [st for h in range(N_HEADS) for st in head_stages(h)]
    return [(2.0, gate_stage)] + _merge_evenly(heads, convs)


def _zero_after(tile):
    bits = lax.bitcast_convert_type(tile, jnp.uint32)
    bits = lax.shift_right_logical(lax.shift_right_logical(bits, jnp.uint32(16)), jnp.uint32(16))
    return lax.bitcast_convert_type(bits, F32)


def _anchor(res, tile):
    z = jnp.concatenate([_zero_after(tile)] * (res.shape[1] // LANES), axis=1)
    return jnp.concatenate([res[0:SUBLANES] + z, res[SUBLANES:]], axis=0)


def _mix_conv(c, p_ref, convw_ref, y_ref, cv_scr, *, cw):
    L = p_ref.shape[0]
    cols = slice(c * LANES, (c + 1) * LANES)
    w = convw_ref[:, cols]
    prev = cv_scr[:, cols]
    seen = jnp.zeros((SUBLANES, LANES), F32)
    for r in range(L // CONV_ROWS):
        rows = slice(r * CONV_ROWS, (r + 1) * CONV_ROWS)
        cx = p_ref[rows, c * LANES:(c + 1) * LANES].astype(F32)
        cb = p_ref[rows, cw + c * LANES:cw + (c + 1) * LANES].astype(F32)
        cc = p_ref[rows, 2 * cw + c * LANES:2 * cw + (c + 1) * LANES].astype(F32)
        u = cc * cx
        ext = jnp.concatenate([prev, u], axis=0)
        s1 = pltpu.roll(ext, 1, 0)[SUBLANES:]
        s2 = pltpu.roll(ext, 2, 0)[SUBLANES:]
        y_conv = cb * (w[0:1, :] * s2 + w[1:2, :] * s1 + w[2:3, :] * u)
        y_ref[rows, cols] = y_conv.astype(y_ref.dtype)
        prev = u[CONV_ROWS - SUBLANES:]
        seen = seen + y_conv[0:SUBLANES]
    cv_scr[:, cols] = prev
    return seen


def _mix_gates(gate_ref, bias_ref, m_scr):
    L = gate_ref.shape[0]
    a = gate_ref[...] + bias_ref[...]
    lane = lax.broadcasted_iota(jnp.int32, a.shape, 1)
    gl = jnp.where(lane < N_HEADS, a, jax.nn.log_sigmoid(a))
    i0 = lax.broadcasted_iota(jnp.int32, (L, L), 0)
    i1 = lax.broadcasted_iota(jnp.int32, (L, L), 1)
    tril = (i0 >= i1).astype(BF16)
    hi, mid, lo = _split3(gl)
    bc = (jnp.dot(tril, hi, preferred_element_type=F32)
          + jnp.dot(tril, mid, preferred_element_type=F32)
          + jnp.dot(tril, lo, preferred_element_type=F32))
    bsh = pltpu.roll(bc, LANES - N_HEADS, 1)
    ab = jnp.where(lane < N_HEADS, gl - bsh, 0.0)
    b_rows = bsh.T
    b_last = bsh[L - 1:L, :]
    m_prev_v = m_scr[0:1, :]
    m_new_v = b_last + jnp.maximum(m_prev_v, jnp.max(ab, axis=0, keepdims=True))
    decay_v = jnp.exp(b_last + m_prev_v - m_new_v)
    wgt = jnp.exp(ab + (b_last - m_new_v))
    m_scr[...] = jnp.broadcast_to(m_new_v, m_scr.shape)
    return dict(ab=ab, b_rows=b_rows, m_prev_v=m_prev_v, decay_v=decay_v, wgt=wgt,
                keep=i0 <= i1)


def _head_cols(h, *, cw, hd):
    mw = N_HEADS * hd
    return tuple(slice(3 * cw + i * mw + h * hd, 3 * cw + i * mw + (h + 1) * hd) for i in range(4))


def _head_scores(h, p_ref, ct_scr, shared, *, cw, hd):
    qs, ks, _, _ = _head_cols(h, cw=cw, hd=hd)
    q = p_ref[:, qs]
    k = p_ref[:, ks]
    b_row = shared["b_rows"][h:h + 1, :]
    a_col = shared["ab"][:, h:h + 1]
    m_prev = shared["m_prev_v"][:, h:h + 1]
    m_inter = b_row + m_prev
    log_d = jnp.where(shared["keep"], a_col + b_row, -jnp.inf)
    m_row = jnp.maximum(m_inter, jnp.max(log_d, axis=0, keepdims=True))
    scores = _dot_t(k, q) * jnp.exp(log_d - m_row)
    return dict(scores=scores, m_row=m_row, inter=jnp.exp(m_inter - m_row), ctn=ct_scr[h])


def _head_output(h, p_ref, ghead_ref, y_ref, local, *, cw, hd):
    qs, _, vs, os_ = _head_cols(h, cw=cw, hd=hd)
    q = p_ref[:, qs]
    o = p_ref[:, os_].astype(F32)
    scores, m_row, inter, ctn = local["scores"], local["m_row"], local["inter"], local["ctn"]
    qc = _dot_t(ctn.astype(BF16), q)
    v_t = p_ref[:, vs].astype(F32).T.astype(BF16)
    local["v_t"] = v_t
    num = jnp.dot(v_t, scores.astype(BF16), preferred_element_type=F32) + inter * qc[0:hd]
    den = jnp.sum(scores, axis=0, keepdims=True) + inter * qc[hd:hd + 1]
    floor = (hd ** 0.5) * jnp.exp(-m_row)
    h_t = num * (1.0 / jnp.maximum(jnp.abs(den), floor))
    hn_t = h_t * lax.rsqrt(jnp.mean(h_t * h_t, axis=0, keepdims=True) + EPS)
    y_h = jax.nn.sigmoid(o) * (hn_t.T * ghead_ref[:, h * hd:(h + 1) * hd])
    y_ref[:, cw + h * hd:cw + (h + 1) * hd] = y_h.astype(y_ref.dtype)
    return y_h[0:SUBLANES, 0:LANES]


def _head_state(h, p_ref, ct_scr, shared, local, *, cw, hd):
    _, ks, _, _ = _head_cols(h, cw=cw, hd=hd)
    ctn = local["ctn"]
    decay = shared["decay_v"][:, h:h + 1]
    kw = p_ref[:, ks].astype(F32) * shared["wgt"][:, h:h + 1]
    c_new = decay * ctn[0:hd] + jnp.dot(local["v_t"], kw.astype(BF16), preferred_element_type=F32)
    ct_scr[h, 0:hd, :] = c_new
    ct_scr[h, hd:hd + 1, :] = decay * ctn[hd:hd + 1] + jnp.sum(kw, axis=0, keepdims=True)
    return c_new[0:SUBLANES, 0:LANES]


def _front_kernel(x_ref, g_ref, wt_ref, wgt_ref, convw_ref, bias_ref, ghead_ref, *rest,
                  n_cast, nc, n_main, tn, cw, hd):
    cast_in, (y_ref, *cast_out) = rest[:n_cast], rest[n_cast:2 * n_cast + 1]
    u_a, u_b, p_a, p_b, g_a, g_b, cv_scr, ct_scr, m_scr = rest[2 * n_cast + 1:]
    s = pl.program_id(0)

    pending = []

    def done(tile):
        pending.append(tile)

    def anchored(res):
        if not pending:
            return res
        tile = functools.reduce(lambda a, b: a + b, pending)
        pending.clear()
        return _anchor(res, tile)

    def norm_stages(u_scr):
        def norm(r):
            rows = slice(r * NORM_ROWS, (r + 1) * NORM_ROWS)
            x = x_ref[rows, :]
            u = x * _rms_scale(x) * g_ref[...]
            u_scr[rows, :] = u.astype(BF16)
            done(u[0:SUBLANES, 0:LANES])

        return [(0.3, functools.partial(norm, r)) for r in range(x_ref.shape[0] // NORM_ROWS)]

    def project_stages(u_scr, p_scr, g_scr):
        def gate_cols():
            g_scr[...] = _dot_t(u_scr[...], wgt_ref[...])

        def main_cols(c):
            cols = slice(c * tn, (c + 1) * tn)
            p_scr[:, cols] = anchored(_dot_t(u_scr[...], wt_ref[cols, :])).astype(BF16)

        return ([(0.5, gate_cols)]
                + [(1.0, functools.partial(main_cols, c)) for c in range(n_main // tn)])

    def mix_stages(p_scr, g_scr):
        def cast_stage():
            for src, dst in zip(cast_in, cast_out):
                dst[...] = src[...].astype(BF16)

        return [(0.5, cast_stage)] + _mix_stages(p_scr, g_scr, convw_ref, bias_ref, ghead_ref, y_ref,
                                          cv_scr, ct_scr, m_scr, done, cw=cw, hd=hd)

    def run_interleaved(*streams):
        *vector_streams, matmul_stream = streams
        merged = []
        for st in vector_streams:
            merged = _merge_evenly(merged, st)
        n_tail = len(matmul_stream) // 8 if merged else 0
        split = len(matmul_stream) - n_tail
        merged = _merge_evenly(matmul_stream[:split], merged) + matmul_stream[split:]
        pending.clear()
        for _, stage in merged:
            stage()
        pending.clear()

    def zero(*refs):
        for ref in refs:
            ref[...] = jnp.zeros_like(ref)

    @pl.when(s == 0)
    def _():
        zero(u_a, u_b, p_a, p_b, g_a, g_b, cv_scr, ct_scr, m_scr)

    @pl.when(jnp.logical_and(s >= 2, lax.rem(s - 2, nc) == 0))
    def _():
        zero(cv_scr, ct_scr, m_scr)

    even = lax.rem(s, 2) == 0

    @pl.when(even)
    def _():
        run_interleaved(mix_stages(p_a, g_a), norm_stages(u_a), project_stages(u_b, p_b, g_b))

    @pl.when(jnp.logical_not(even))
    def _():
        run_interleaved(mix_stages(p_b, g_b), norm_stages(u_b), project_stages(u_a, p_a, g_a))


def _front(x2, g, w_t, w_gate_t, conv_w, bias, g_head, to_cast, *, n_main, chunk, seq, tn, cw, hd):
    m, d = x2.shape
    mw = N_HEADS * hd
    n_chunks = m // chunk
    nc = seq // chunk
    assert n_chunks % 2 == 0 and n_main % tn == 0 and n_main <= w_t.shape[0]
    slabs = [(w.shape[0] // n_chunks, w.shape[1]) for w in to_cast]
    assert all(r * n_chunks == w.shape[0] and r % BF16_ROWS == 0
               for (r, _), w in zip(slabs, to_cast))
    norm_blk = lambda s: (jnp.minimum(s, n_chunks - 1), 0)
    mix_blk = lambda s: (jnp.clip(s - 2, 0, n_chunks - 1), 0)
    whole = lambda s: (0, 0)
    kern = functools.partial(_front_kernel, n_cast=len(to_cast), nc=nc, n_main=n_main, tn=tn,
                             cw=cw, hd=hd)
    return pl.pallas_call(
        kern,
        name="front",
        grid=(n_chunks + 2,),
        in_specs=[
            pl.BlockSpec((chunk, d), norm_blk),
            pl.BlockSpec((1, d), whole),
            pl.BlockSpec(w_t.shape, whole, pipeline_mode=pl.Buffered(1)),
            pl.BlockSpec((LANES, d), whole, pipeline_mode=pl.Buffered(1)),
            pl.BlockSpec((CONV_K, cw), whole),
            pl.BlockSpec((1, LANES), whole),
            pl.BlockSpec((1, mw), whole),
        ] + [pl.BlockSpec(slab, mix_blk) for slab in slabs],
        out_specs=[pl.BlockSpec((chunk, cw + mw), mix_blk)]
        + [pl.BlockSpec(slab, mix_blk) for slab in slabs],
        out_shape=[jax.ShapeDtypeStruct((m, cw + mw), BF16)]
        + [jax.ShapeDtypeStruct(w.shape, BF16) for w in to_cast],
        scratch_shapes=[
            pltpu.VMEM((chunk, d), BF16),
            pltpu.VMEM((chunk, d), BF16),
            pltpu.VMEM((chunk, n_main), BF16),
            pltpu.VMEM((chunk, n_main), BF16),
            pltpu.VMEM((chunk, LANES), F32),
            pltpu.VMEM((chunk, LANES), F32),
            pltpu.VMEM((SUBLANES, cw), F32),
            pltpu.VMEM((N_HEADS, hd + BF16_ROWS, hd), F32),
            pltpu.VMEM((SUBLANES, LANES), F32),
        ],
        compiler_params=pltpu.CompilerParams(
            dimension_semantics=("arbitrary",),
            vmem_limit_bytes=VMEM_LIMIT),
    )(x2, g, w_t, w_gate_t, conv_w, bias, g_head, *to_cast)


def _out_proj_kernel(y_ref, w_ref, x_ref, g_ref, o_ref):
    t = jnp.dot(y_ref[...], w_ref[...], preferred_element_type=F32)
    o_ref[...] = x_ref[...] + t * _rms_scale(t) * g_ref[...]


def _out_proj(y, w_out, x2, g, *, tm, vmem_limit):
    m, d = x2.shape
    e = y.shape[1]
    return pl.pallas_call(
        _out_proj_kernel,
        name="out_proj",
        grid=(m // tm,),
        in_specs=[
            pl.BlockSpec((tm, e), lambda i: (i, 0)),
            pl.BlockSpec((e, d), lambda i: (0, 0), pipeline_mode=pl.Buffered(1)),
            pl.BlockSpec((tm, d), lambda i: (i, 0)),
            pl.BlockSpec((1, d), lambda i: (0, 0)),
        ],
        out_specs=pl.BlockSpec((tm, d), lambda i: (i, 0)),
        out_shape=jax.ShapeDtypeStruct((m, d), F32),
        compiler_params=pltpu.CompilerParams(
            dimension_semantics=("arbitrary",),
            vmem_limit_bytes=vmem_limit),
    )(y, w_out, x2, g)


def _mlp_kernel(x_ref, g1_ref, w1_hbm, w2_hbm, g2_ref, o_ref, z_scr, w1_buf, w2_buf, sem,
                *, tf, nf, sub, mid_sub):
    i = pl.program_id(0)
    last_i = pl.num_programs(0) - 1
    n_sub = x_ref.shape[0] // sub

    def w_copies(f, slot):
        return (pltpu.make_async_copy(w1_hbm.at[:, pl.ds(f * tf, tf)], w1_buf.at[slot],
                                      sem.at[0, slot]),
                pltpu.make_async_copy(w2_hbm.at[pl.ds(f * tf, tf), :], w2_buf.at[slot],
                                      sem.at[1, slot]))

    def start(f, slot):
        for priority, c in enumerate(w_copies(f, slot)):
            c.start(priority=priority)

    def wait(f, slot):
        for c in w_copies(f, slot):
            c.wait()

    def ffn(z, slot):
        h = jnp.dot(z, w1_buf[slot], preferred_element_type=F32)
        h = jnp.square(jnp.maximum(h, 0.0)).astype(BF16)
        return jnp.dot(h, w2_buf[slot], preferred_element_type=F32)

    @pl.when(i == 0)
    def _():
        start(0, 0)

    start(1, 1)
    wait(0, 0)
    for r in range(n_sub):
        rows = slice(r * sub, (r + 1) * sub)
        x = x_ref[rows, :]
        z = (x * _rms_scale(x) * g1_ref[...]).astype(BF16)
        z_scr[rows, :] = z
        o_ref[rows, :] = ffn(z, 0)

    def middle(f, carry):
        slot = lax.rem(f, 2)
        start(f + 1, 1 - slot)
        wait(f, slot)
        for r in range(x_ref.shape[0] // mid_sub):
            rows = slice(r * mid_sub, (r + 1) * mid_sub)
            o_ref[rows, :] += ffn(z_scr[rows, :], slot)
        return carry

    lax.fori_loop(1, nf - 1, middle, 0)

    @pl.when(i < last_i)
    def _():
        start(0, 0)

    last_slot = (nf - 1) % 2
    wait(nf - 1, last_slot)
    for r in range(n_sub):
        rows = slice(r * sub, (r + 1) * sub)
        t = o_ref[rows, :] + ffn(z_scr[rows, :], last_slot)
        o_ref[rows, :] = x_ref[rows, :] + t * _rms_scale(t) * g2_ref[...]


def _mlp(x1, g1, w1, w2, g2, *, tm, tf, sub, mid_sub, vmem_limit):
    m, d = x1.shape
    dff = w1.shape[1]
    nf = dff // tf
    assert nf >= 3 and nf % 2 == 0
    return pl.pallas_call(
        functools.partial(_mlp_kernel, tf=tf, nf=nf, sub=sub, mid_sub=mid_sub),
        name="mlp",
        grid=(m // tm,),
        in_specs=[
            pl.BlockSpec((tm, d), lambda i: (i, 0)),
            pl.BlockSpec((1, d), lambda i: (0, 0)),
            pl.BlockSpec(memory_space=pl.ANY),
            pl.BlockSpec(memory_space=pl.ANY),
            pl.BlockSpec((1, d), lambda i: (0, 0)),
        ],
        out_specs=pl.BlockSpec((tm, d), lambda i: (i, 0)),
        out_shape=jax.ShapeDtypeStruct((m, d), F32),
        scratch_shapes=[
            pltpu.VMEM((tm, d), BF16),
            pltpu.VMEM((2, d, tf), BF16),
            pltpu.VMEM((2, tf, d), BF16),
            pltpu.SemaphoreType.DMA((2, 2)),
        ],
        compiler_params=pltpu.CompilerParams(
            dimension_semantics=("arbitrary",),
            vmem_limit_bytes=vmem_limit),
    )(x1, g1, w1, w2, g2)


def _layer(x2, g_pre_mix, g_post_mix, g_pre_mlp, g_post_mlp, w_in, conv_w, b_i, b_f,
           g_head, w_out, w_mlp1, w_mlp2, *, seq):
    d = x2.shape[1]
    cw = conv_w.shape[1]
    mw = g_head.shape[0]
    hd = mw // N_HEADS
    main = 3 * cw + 4 * mw
    n_gate = 2 * N_HEADS

    w_t = jnp.swapaxes(w_in, 0, 1).astype(BF16)
    w_gate_t = jnp.pad(w_t[main:main + n_gate], ((0, LANES - n_gate), (0, 0)))
    bias = jnp.pad(jnp.concatenate([b_i, b_f]).astype(F32), (0, LANES - n_gate)).reshape(1, LANES)

    t = TILES
    y, w1b, w2b, w_out_b = _front(
        x2, g_pre_mix.reshape(1, d), w_t, w_gate_t, conv_w.astype(F32), bias,
        g_head.reshape(1, mw).astype(F32), [w_mlp1, w_mlp2, w_out],
        n_main=main, chunk=t["chunk"], seq=seq, tn=t["proj_cols"], cw=cw, hd=hd)
    x1 = _out_proj(y, w_out_b, x2, g_post_mix.reshape(1, d), tm=t["out_rows"],
                   vmem_limit=VMEM_LIMIT)
    return _mlp(x1, g_pre_mlp.reshape(1, d), w1b, w2b, g_post_mlp.reshape(1, d),
                tm=t["mlp_rows"], tf=t["mlp_hidden"], sub=t["mlp_edge_sub"],
                mid_sub=t["mlp_mid_sub"], vmem_limit=VMEM_LIMIT)


def kernel(x, g_pre_mix, g_post_mix, g_pre_mlp, g_post_mlp, w_in, conv_w, b_i, b_f, g_head,
           w_out, w_mlp1, w_mlp2):
    batch, seq, d = x.shape
    x2 = x.reshape(batch * seq, d)
    for layer in range(w_in.shape[0]):
        x2 = _layer(x2, g_pre_mix[layer], g_post_mix[layer], g_pre_mlp[layer], g_post_mlp[layer],
                    w_in[layer], conv_w[layer], b_i[layer], b_f[layer], g_head[layer],
                    w_out[layer], w_mlp1[layer], w_mlp2[layer], seq=seq)
    return x2.reshape(batch, seq, d)
```
